```python
import math
import jax, jax.numpy as jnp
from jax import lax
import numpy as np

D_MODEL = 1024
BATCH = 16
SEQ = 2048
DEPTH = 2

CTX_LEN = 256
GRID_W = 64
N_MIXERS = 4
GROUP_W = D_MODEL // N_MIXERS
D_MIX = GROUP_W * N_MIXERS
HEAD_DIM = 64
HG_DK = 64
HG_DV = 64
HG_HEADS = GROUP_W // HG_DV
HG_CHUNK = 64
GA_HEADS = GROUP_W // HEAD_DIM
GA_KV = GA_HEADS // 2
DF_V = 64
DF_QK = DF_V // 2
DF_HEADS = GROUP_W // DF_V
WN_HEADS = GROUP_W // HEAD_DIM
WN_KV = WN_HEADS // 2
WINDOW = 128
Q_BLOCK = 128
ROPE_THETA = 10000.0
LN_EPS = 1e-5
RMS_EPS = 1e-6

IN_WIDTHS = (
    HG_HEADS * HG_DK, HG_HEADS * HG_DV, HG_HEADS * HG_DK, HG_HEADS * HG_DK,
    GA_HEADS * HEAD_DIM, GA_KV * HEAD_DIM, GA_KV * HEAD_DIM,
    DF_HEADS * 2 * DF_QK, DF_HEADS * 2 * DF_QK, DF_HEADS * DF_V,
    WN_HEADS * HEAD_DIM, WN_KV * HEAD_DIM, WN_KV * HEAD_DIM,
    D_MIX,
)
IN_WIDTH = sum(IN_WIDTHS)
SPLIT_AT = tuple(int(v) for v in np.cumsum(IN_WIDTHS)[:-1])

kernel_name = "hybrid_parallel_groups_dit_block"


def rms_norm(x, g):
    xf = x.astype(jnp.float32)
    y = xf * lax.rsqrt(jnp.mean(xf * xf, axis=-1, keepdims=True) + RMS_EPS)
    return (y * g).astype(x.dtype)


def layer_norm(x, g, b):
    xf = x.astype(jnp.float32)
    mu = jnp.mean(xf, axis=-1, keepdims=True)
    var = jnp.mean(jnp.square(xf - mu), axis=-1, keepdims=True)
    return ((xf - mu) * lax.rsqrt(var + LN_EPS) * g + b).astype(x.dtype)


def split_heads(t, n_heads):
    return t.reshape(t.shape[:-1] + (n_heads, t.shape[-1] // n_heads))


def merge_heads(t):
    return t.reshape(t.shape[:2] + (-1,))


def axial_rope(n_tok, dim):
    rows = n_tok // GRID_W
    row = jnp.repeat(jnp.arange(rows, dtype=jnp.float32), GRID_W)
    col = jnp.broadcast_to(jnp.arange(GRID_W, dtype=jnp.float32), (rows, GRID_W)).reshape(-1)
    d_axis = dim // 2
    inv = ROPE_THETA ** (-jnp.arange(0, d_axis, 2, dtype=jnp.float32) / d_axis)
    ang_r = row[:, None] * inv
    ang_c = col[:, None] * inv
    ang = jnp.concatenate([ang_r, ang_r, ang_c, ang_c], axis=-1)
    return jnp.cos(ang), jnp.sin(ang)


def apply_rope(x, rope):
    cos, sin = rope
    a1, a2, b1, b2 = jnp.split(x, 4, axis=-1)
    rot = jnp.concatenate([-a2, a1, -b2, b1], axis=-1)
    return (x * cos[:, None, :] + rot * sin[:, None, :]).astype(x.dtype)


def sweep_query_blocks(fn, q):
    b, s = q.shape[0], q.shape[1]
    nb = s // Q_BLOCK
    qb = jnp.moveaxis(q.reshape((b, nb, Q_BLOCK) + q.shape[2:]), 1, 0)
    out = lax.map(fn, qb)
    return jnp.moveaxis(out, 0, 1).reshape((b, s) + out.shape[3:])


def hgrn_forget(z, lb):
    f = lb + (1.0 - lb) * jax.nn.sigmoid(z.astype(jnp.float32))
    return (1.0 - f).astype(z.dtype), jnp.log(f)


def gla_chunk_scan(q, k, v, logf, s0):
    b, t, h, _ = q.shape
    dv = v.shape[-1]
    nc = t // HG_CHUNK

    def to_chunks(a):
        return jnp.moveaxis(a.reshape(b, nc, HG_CHUNK, h, a.shape[-1]), 1, 0)

    incl = jnp.tril(jnp.ones((HG_CHUNK, HG_CHUNK), dtype=bool))

    def step(s, xs):
        qc, kc, vc, fc = xs
        qf, kf, vf = qc.astype(jnp.float32), kc.astype(jnp.float32), vc.astype(jnp.float32)
        bcum = jnp.cumsum(fc, axis=1)
        o_inter = jnp.einsum('bthk,bhkv->bthv', qf * jnp.exp(bcum), s)
        diff = bcum[:, :, None] - bcum[:, None, :]
        decay = jnp.exp(jnp.where(incl[None, :, :, None, None], diff, -jnp.inf))
        att = jnp.einsum('bthk,bshk,btshk->bhts', qf, kf, decay)
        o_intra = jnp.einsum('bhts,bshv->bthv', att, vf)
        blast = bcum[:, -1]
        s_new = jnp.exp(blast)[..., None] * s + jnp.einsum(
            'bshk,bshv->bhkv', kf * jnp.exp(blast[:, None] - bcum), vf)
        return s_new, (o_inter + o_intra).astype(v.dtype)

    s_fin, o = lax.scan(step, s0, (to_chunks(q), to_chunks(k), to_chunks(v), to_chunks(logf)))
    return jnp.moveaxis(o, 0, 1).reshape(b, t, h, dv), s_fin


def gla_final_state(k, v, logf):
    after = lax.cumsum(logf, axis=1, reverse=True) - logf
    return jnp.einsum('bshk,bshv->bhkv', k.astype(jnp.float32) * jnp.exp(after), v.astype(jnp.float32))


def hgrn2_mixer(q, i, f_fwd, f_bwd, qc, ic, fc_fwd, fc_bwd, lb, norm_g, ctx_out):
    hs = lambda t: split_heads(t, HG_HEADS)
    flip = lambda t: jnp.flip(t, axis=1)
    q, qc = jax.nn.silu(hs(q)), jax.nn.silu(hs(qc))
    i, ic = hs(i), hs(ic)
    lb_f = lb[0].reshape(HG_HEADS, HG_DK)
    lb_b = lb[1].reshape(HG_HEADS, HG_DK)
    k_f, lf_f = hgrn_forget(hs(f_fwd), lb_f)
    k_b, lf_b = hgrn_forget(hs(f_bwd), lb_b)
    kc_f, lfc_f = hgrn_forget(hs(fc_fwd), lb_f)
    kc_b, lfc_b = hgrn_forget(hs(fc_bwd), lb_b)
    b = qc.shape[0]
    if ctx_out:
        zeros = jnp.zeros((b, HG_HEADS, HG_DK, HG_DV), jnp.float32)
        oc_f, s_cf = gla_chunk_scan(qc, kc_f, ic, lfc_f, zeros)
        oc_b, s_cb = gla_chunk_scan(flip(qc), flip(kc_b), flip(ic), flip(lfc_b), zeros)
        yc = merge_heads(rms_norm(oc_f + flip(oc_b), norm_g))
    else:
        s_cf = gla_final_state(kc_f, ic, lfc_f)
        s_cb = gla_final_state(flip(kc_b), flip(ic), flip(lfc_b))
        yc = None
    o_f, _ = gla_chunk_scan(q, k_f, i, lf_f, s_cf)
    o_b, _ = gla_chunk_scan(flip(q), flip(k_b), flip(i), flip(lf_b), s_cb)
    y = merge_heads(rms_norm(o_f + flip(o_b), norm_g))
    return y, yc


def global_gqa(q, k, v, sink=None):
    b, s, h, d = q.shape
    kv = k.shape[2]
    g = h // kv
    scale = d ** -0.5
    qg = q.reshape(b, s, kv, g, d)

    def block(qb):
        logits = jnp.einsum('bqkgd,btkd->bkgqt', qb, k).astype(jnp.float32) * scale
        if sink is None:
            p = jax.nn.softmax(logits, axis=-1)
        else:
            col = jnp.broadcast_to(sink.astype(jnp.float32).reshape(kv, g)[None, :, :, None, None],
                                   logits.shape[:-1] + (1,))
            p = jax.nn.softmax(jnp.concatenate([logits, col], axis=-1), axis=-1)[..., :-1]
        return jnp.einsum('bkgqt,btkd->bqkgd', p.astype(v.dtype), v)

    return sweep_query_blocks(block, qg).reshape(b, s, h, d)


def diff_attention(q, k, v, lam, lam_init, subln_g):
    scale = q.shape[-1] ** -0.5

    def block(qb):
        logits = jnp.einsum('bqhmd,bthmd->bhmqt', qb, k).astype(jnp.float32) * scale
        p = jax.nn.softmax(logits, axis=-1)
        w = p[:, :, 0] - lam * p[:, :, 1]
        return jnp.einsum('bhqt,bthd->bqhd', w.astype(v.dtype), v)

    o = sweep_query_blocks(block, q)
    return rms_norm(o, subln_g) * (1.0 - lam_init)


def window_gqa_latent(q, k, v, k_ctx, v_ctx, sink):
    b, s, h, d = q.shape
    kv = k.shape[2]
    g = h // kv
    nb = s // Q_BLOCK
    scale = d ** -0.5

    def banded(t):
        tp = jnp.pad(t, ((0, 0), (Q_BLOCK, Q_BLOCK), (0, 0), (0, 0)))
        tb = tp.reshape(b, nb + 2, Q_BLOCK, kv, d)
        return jnp.concatenate([tb[:, :-2], tb[:, 1:-1], tb[:, 2:]], axis=2)

    kb, vb = banded(k), banded(v)
    qb = q.reshape(b, nb, Q_BLOCK, kv, g, d)
    blk = jnp.arange(nb)[:, None, None]
    qpos = blk * Q_BLOCK + jnp.arange(Q_BLOCK)[None, :, None]
    kpos = (blk - 1) * Q_BLOCK + jnp.arange(3 * Q_BLOCK)[None, None, :]
    allowed = (jnp.abs(kpos - qpos) <= WINDOW) & (kpos >= 0) & (kpos < s)
    lw = jnp.einsum('bnqkgd,bnskd->bnkgqs', qb, kb).astype(jnp.float32) * scale
    lw = jnp.where(allowed[None, :, None, None], lw, -jnp.inf)
    lc = jnp.einsum('bnqkgd,bskd->bnkgqs', qb, k_ctx).astype(jnp.float32) * scale
    ls = jnp.broadcast_to(sink.astype(jnp.float32).reshape(kv, g)[None, None, :, :, None, None],
                          lw.shape[:-1] + (1,))
    p = jax.nn.softmax(jnp.concatenate([lw, lc, ls], axis=-1), axis=-1).astype(v.dtype)
    n_w = 3 * Q_BLOCK
    n_c = k_ctx.shape[1]
    o = (jnp.einsum('bnkgqs,bnskd->bnqkgd', p[..., :n_w], vb)
         + jnp.einsum('bnkgqs,bskd->bnqkgd', p[..., n_w:n_w + n_c], v_ctx))
    return o.reshape(b, s, h, d)


def mixer_layer(x, cx, c, c_ctx, w_in, w_out, w_ada, b_ada, ln_g, ln_b, lb, hg_norm_g,
                ga_qn, ga_kn, df_lam, df_subln_g, wn_sink, rope64, rope32, layer_idx, ctx_out):
    alpha = (2.0 * DEPTH) ** 0.25
    shift, scale, gate = jnp.split(jax.nn.silu(c) @ w_ada + b_ada, 3, axis=-1)
    shift_c, scale_c, gate_c = jnp.split(jax.nn.silu(c_ctx) @ w_ada + b_ada, 3, axis=-1)
    h = x * (1.0 + scale[:, None]) + shift[:, None]
    hc = cx * (1.0 + scale_c) + shift_c
    (a_q, a_i, a_ff, a_fb, b_q, b_k, b_v, c_q, c_k, c_v, d_q, d_k, d_v, g_lat) = jnp.split(h @ w_in, SPLIT_AT, axis=-1)
    (a_qc, a_ic, a_ffc, a_fbc, b_qc, b_kc, b_vc, c_qc, c_kc, c_vc, d_qc, d_kc, d_vc, g_ctx) = jnp.split(hc @ w_in, SPLIT_AT, axis=-1)

    y_a, yc_a = hgrn2_mixer(a_q, a_i, a_ff, a_fb, a_qc, a_ic, a_ffc, a_fbc, lb, hg_norm_g, ctx_out)

    qB = apply_rope(rms_norm(split_heads(b_q, GA_HEADS), ga_qn), rope64)
    kB = apply_rope(rms_norm(split_heads(b_k, GA_KV), ga_kn), rope64)
    vB = split_heads(b_v, GA_KV)
    kBc = rms_norm(split_heads(b_kc, GA_KV), ga_kn)
    vBc = split_heads(b_vc, GA_KV)
    y_b = merge_heads(global_gqa(qB, jnp.concatenate([kB, kBc], 1), jnp.concatenate([vB, vBc], 1)))

    bsz, n = x.shape[0], x.shape[1]
    n_ctx = cx.shape[1]
    def diff_qk(t, length, rotate):
        t = t.reshape(bsz, length, DF_HEADS * 2, DF_QK)
        if rotate:
            t = apply_rope(t, rope32)
        return t.reshape(bsz, length, DF_HEADS, 2, DF_QK)
    lam_init = 0.8 - 0.6 * math.exp(-0.3 * layer_idx)
    lamf = df_lam.astype(jnp.float32)
    lam = jnp.exp(jnp.sum(lamf[0] * lamf[1])) - jnp.exp(jnp.sum(lamf[2] * lamf[3])) + lam_init
    qC, kC, vC = diff_qk(c_q, n, True), diff_qk(c_k, n, True), split_heads(c_v, DF_HEADS)
    kCc, vCc = diff_qk(c_kc, n_ctx, False), split_heads(c_vc, DF_HEADS)
    y_c = merge_heads(diff_attention(qC, jnp.concatenate([kC, kCc], 1), jnp.concatenate([vC, vCc], 1),
                                     lam, lam_init, df_subln_g))

    qD = apply_rope(split_heads(d_q, WN_HEADS), rope64)
    kD = apply_rope(split_heads(d_k, WN_KV), rope64)
    vD = split_heads(d_v, WN_KV)
    kDc, vDc = split_heads(d_kc, WN_KV), split_heads(d_vc, WN_KV)
    y_d = merge_heads(window_gqa_latent(qD, kD, vD, kDc, vDc, wn_sink))

    y = jnp.concatenate([y_a, y_b, y_c, y_d], axis=-1) * jax.nn.silu(g_lat)
    x_new = layer_norm(alpha * x + gate[:, None] * (y @ w_out), ln_g, ln_b)
    if not ctx_out:
        return x_new, None

    yc_b = merge_heads(global_gqa(rms_norm(split_heads(b_qc, GA_HEADS), ga_qn), kBc, vBc))
    yc_c = merge_heads(diff_attention(diff_qk(c_qc, n_ctx, False), kCc, vCc, lam, lam_init, df_subln_g))
    yc_d = merge_heads(global_gqa(split_heads(d_qc, WN_HEADS), kDc, vDc, wn_sink))
    yc = jnp.concatenate([yc_a, yc_b, yc_c, yc_d], axis=-1) * jax.nn.silu(g_ctx)
    cx_new = layer_norm(alpha * cx + gate_c * (yc @ w_out), ln_g, ln_b)
    return x_new, cx_new


def setup_inputs(seed: int = 0) -> dict:
    key = jax.random.key(seed)
    ks = jax.random.split(key, 17)
    f32 = jnp.float32
    nrm = lambda k, shape, s: jax.random.normal(k, shape, f32) * s
    beta = (8.0 * DEPTH) ** -0.25
    return {
        "x": nrm(ks[0], (BATCH, SEQ, D_MODEL), 1.0),
        "c": nrm(ks[1], (BATCH, D_MODEL), 1.0),
        "ctx": nrm(ks[2], (BATCH, CTX_LEN, D_MODEL), 1.0),
        "c_ctx": nrm(ks[3], (D_MODEL,), 1.0),
        "w_in": nrm(ks[4], (DEPTH, D_MODEL, IN_WIDTH), D_MODEL ** -0.5),
        "w_out": nrm(ks[5], (DEPTH, D_MIX, D_MODEL), (D_MIX ** -0.5) * beta),
        "w_ada": nrm(ks[6], (DEPTH, D_MODEL, 3 * D_MODEL), 0.5 * D_MODEL ** -0.5),
        "b_ada": nrm(ks[7], (DEPTH, 3 * D_MODEL), 0.02),
        "ln_g": 1.0 + nrm(ks[8], (DEPTH, D_MODEL), 0.02),
        "ln_b": nrm(ks[9], (DEPTH, D_MODEL), 0.02),
        "hg_lb_logits": nrm(ks[10], (DEPTH, 2, HG_HEADS * HG_DK), 0.1),
        "hg_norm_g": 1.0 + nrm(ks[11], (DEPTH, HG_DV), 0.02),
        "ga_q_norm_g": 1.0 + nrm(ks[12], (DEPTH, HEAD_DIM), 0.02),
        "ga_k_norm_g": 1.0 + nrm(ks[13], (DEPTH, HEAD_DIM), 0.02),
        "df_lambda": nrm(ks[14], (DEPTH, 4, DF_QK), 0.1),
        "df_subln_g": 1.0 + nrm(ks[15], (DEPTH, DF_V), 0.02),
        "wn_sink": nrm(ks[16], (DEPTH, WN_HEADS), 0.5),
    }


def reference(x, c, ctx, c_ctx, w_in, w_out, w_ada, b_ada, ln_g, ln_b, hg_lb_logits, hg_norm_g,
              ga_q_norm_g, ga_k_norm_g, df_lambda, df_subln_g, wn_sink):
    n = x.shape[1]
    rope64 = axial_rope(n, HEAD_DIM)
    rope32 = axial_rope(n, DF_QK)
    lb_sm = jax.nn.softmax(hg_lb_logits.astype(jnp.float32), axis=0)
    lower_bounds = jnp.cumsum(lb_sm, axis=0) - lb_sm[0]
    cx = ctx
    for l in range(DEPTH):
        x, cx = mixer_layer(x, cx, c, c_ctx, w_in[l], w_out[l], w_ada[l], b_ada[l], ln_g[l], ln_b[l],
                            lower_bounds[l], hg_norm_g[l], ga_q_norm_g[l], ga_k_norm_g[l], df_lambda[l],
                            df_subln_g[l], wn_sink[l], rope64, rope32, l, l < DEPTH - 1)
    return x
```

```python
import functools
import math

import numpy as np
import jax
import jax.numpy as jnp
from jax import lax
from jax.experimental import pallas as pl
from jax.experimental.pallas import tpu as pltpu

F32 = jnp.float32
BF16 = jnp.bfloat16

D_MODEL = 1024
GROUP_W = 256
HEAD_DIM = 64
DF_QK = 32
GRID_W = 64
WINDOW = 128
ROPE_THETA = 10000.0
LN_EPS = 1e-5
RMS_EPS = 1e-6
HG_CHUNK = 64
HG_SUB = 8

LANES = 128
TM = 256
MOD_ROWS = 24

A_W = 4 * GROUP_W
ATT_W = 14 * LANES
ATT_OFF = A_W
GATE_OFF = A_W + ATT_W
IN_W = GATE_OFF + D_MODEL
B_Q, B_K, B_V = 0, 2, 3
C_Q, C_K, C_V = 4, 6, 8
D_Q, D_K, D_V = 10, 12, 13
HEAD_PERM = (0, 2, 1, 3)

NT = (((1,), (1,)), ((), ()))
TN = (((0,), (0,)), ((), ()))


def _dot(a, b, dims=None):
    if dims is None:
        return jnp.dot(a, b, preferred_element_type=F32)
    return lax.dot_general(a, b, dims, preferred_element_type=F32)


def _split2(x):
    hi = x.astype(BF16)
    lo = (x - hi.astype(F32)).astype(BF16)
    return hi, lo


def _seg_sum(x, ones_bd):
    hi, lo = _split2(x)
    return _dot(hi, ones_bd) + _dot(lo, ones_bd)


def _block_ones(n, seg):
    r = lax.broadcasted_iota(jnp.int32, (n, n), 0) // seg
    c = lax.broadcasted_iota(jnp.int32, (n, n), 1) // seg
    return jnp.where(r == c, 1.0, 0.0).astype(BF16)


def _silu(x):
    return x * jax.nn.sigmoid(x)


def _ada_kernel(c_ref, w_ref, b_ref, o_ref):
    s = _silu(c_ref[...])
    o_ref[0] = jnp.dot(s, w_ref[0], preferred_element_type=F32, precision=lax.Precision.HIGHEST) + b_ref[0]


def _ada_call(cc, w_ada, b_ada):
    depth = w_ada.shape[0]
    n3 = w_ada.shape[2]
    nb = n3 // D_MODEL
    return pl.pallas_call(
        _ada_kernel,
        grid=(depth, nb),
        in_specs=[
            pl.BlockSpec((MOD_ROWS, D_MODEL), lambda l, j: (0, 0)),
            pl.BlockSpec((1, D_MODEL, D_MODEL), lambda l, j: (l, 0, j)),
            pl.BlockSpec((1, 1, D_MODEL), lambda l, j: (l, 0, j)),
        ],
        out_specs=pl.BlockSpec((1, MOD_ROWS, D_MODEL), lambda l, j: (l, 0, j)),
        out_shape=jax.ShapeDtypeStruct((depth, MOD_ROWS, n3), F32),
        compiler_params=pltpu.CompilerParams(dimension_semantics=("arbitrary", "arbitrary"),
                                             vmem_limit_bytes=32 * 1024 * 1024),
        name="ada_mod",
    )(cc, w_ada, b_ada.reshape(depth, 1, n3))


def _rope(x, cos, sa, sb, shift):
    up = pltpu.roll(x, LANES - shift, axis=1)
    dn = pltpu.roll(x, shift, axis=1)
    return x * cos + up * sa + dn * sb


def _inproj_kernel(nct, x_ref, c_ref, sh_ref, sc_ref, w_ref, qn_ref, kn_ref,
                   c64_ref, sa64_ref, sb64_ref, c32_ref, sa32_ref, sb32_ref,
                   za_ref, at_ref, sg_ref, h_scr):
    t = pl.program_id(1)
    sc = 1.0 + sc_ref[0]
    sh = sh_ref[0]

    @pl.when(t < nct)
    def _():
        h_scr[...] = (c_ref[0] * sc + sh).astype(BF16)

    @pl.when(t >= nct)
    def _():
        h_scr[...] = (x_ref[0] * sc + sh).astype(BF16)

    h = h_scr[...]

    def proj(blk0, nblk):
        c0 = ATT_OFF + blk0 * LANES
        return _dot(h, w_ref[:, c0:c0 + nblk * LANES])

    def put(blk, val):
        at_ref[0, :, blk * LANES:(blk + 1) * LANES] = val.astype(BF16)

    def lanes(a, j):
        return a[:, j * LANES:(j + 1) * LANES]

    za_ref[0] = _dot(h, w_ref[:, 0:A_W])

    ones64 = _block_ones(LANES, HEAD_DIM)
    c64, sa64, sb64 = c64_ref[...], sa64_ref[...], sb64_ref[...]
    c32, sa32, sb32 = c32_ref[...], sa32_ref[...], sb32_ref[...]
    q64 = HEAD_DIM // 4
    q32 = DF_QK // 4

    def rms(v, g):
        ms = _seg_sum(v * v, ones64) * (1.0 / HEAD_DIM)
        return v * lax.rsqrt(ms + RMS_EPS) * g

    acc = proj(B_Q, 4)
    for j in range(2):
        put(B_Q + j, _rope(rms(lanes(acc, j), qn_ref[...]), c64, sa64, sb64, q64) * (HEAD_DIM ** -0.5))
    put(B_K, _rope(rms(lanes(acc, 2), kn_ref[...]), c64, sa64, sb64, q64))
    put(B_V, lanes(acc, 3))

    acc = proj(C_Q, 6)
    for j in range(2):
        put(C_Q + j, _rope(lanes(acc, j), c32, sa32, sb32, q32) * (DF_QK ** -0.5))
        put(C_K + j, _rope(lanes(acc, 2 + j), c32, sa32, sb32, q32))
        put(C_V + j, lanes(acc, 4 + j))

    acc = proj(D_Q, 4)
    for j in range(2):
        put(D_Q + j, _rope(lanes(acc, j), c64, sa64, sb64, q64) * (HEAD_DIM ** -0.5))
    put(D_K, _rope(lanes(acc, 2), c64, sa64, sb64, q64))
    put(D_V, lanes(acc, 3))

    sg_ref[0] = _silu(_dot(h, w_ref[:, GATE_OFF:IN_W]))


def _inproj_call(x, cx, modr, layer, wp, qn, kn, tabs):
    bsz, s, _ = x.shape
    lc = cx.shape[1]
    nct, nlt = lc // TM, s // TM
    t_tot = lc + s
    tab_spec = pl.BlockSpec((TM, LANES), lambda b, t: (t, 0))
    vec_spec = pl.BlockSpec((1, LANES), lambda b, t: (0, 0))

    def mod_spec(col):
        return pl.BlockSpec((1, 1, D_MODEL),
                            lambda b, t: (layer * MOD_ROWS + jnp.where(t < nct, MOD_ROWS - 8, b), 0, col))

    return pl.pallas_call(
        functools.partial(_inproj_kernel, nct),
        grid=(bsz, nct + nlt),
        in_specs=[
            pl.BlockSpec((1, TM, D_MODEL), lambda b, t: (b, jnp.maximum(t - nct, 0), 0)),
            pl.BlockSpec((1, TM, D_MODEL), lambda b, t: (b, jnp.minimum(t, nct - 1), 0)),
            mod_spec(0), mod_spec(1),
            pl.BlockSpec((D_MODEL, IN_W), lambda b, t: (0, 0)),
            vec_spec, vec_spec,
            tab_spec, tab_spec, tab_spec, tab_spec, tab_spec, tab_spec,
        ],
        out_specs=[
            pl.BlockSpec((1, TM, A_W), lambda b, t: (b, t, 0)),
            pl.BlockSpec((1, TM, ATT_W), lambda b, t: (b, t, 0)),
            pl.BlockSpec((1, TM, D_MODEL), lambda b, t: (b, t, 0)),
        ],
        out_shape=[
            jax.ShapeDtypeStruct((bsz, t_tot, A_W), F32),
            jax.ShapeDtypeStruct((bsz, t_tot, ATT_W), BF16),
            jax.ShapeDtypeStruct((bsz, t_tot, D_MODEL), F32),
        ],
        scratch_shapes=[pltpu.VMEM((TM, D_MODEL), BF16)],
        compiler_params=pltpu.CompilerParams(dimension_semantics=("arbitrary", "arbitrary"),
                                             vmem_limit_bytes=48 * 1024 * 1024),
        name="in_proj",
    )(x, cx, modr, modr, wp, qn, kn, *tabs)


def _hgrn_levels():
    w = HG_CHUNK // 2
    out = []
    while w >= HG_SUB:
        out.append(w)
        w //= 2
    return tuple(out)


def _hgrn_sum_matrix(rev):
    c = HG_CHUNK
    t = np.arange(c)[:, None]
    j = np.arange(c)[None, :]
    mats = [j <= t, j > t]
    for w in _hgrn_levels():
        m = (t // (2 * w)) * 2 * w + w
        mats.append(np.where(t >= m, (j >= m) & (j <= t), (j > t) & (j <= m - 1)))
    mats.append((j <= t) & (j // HG_SUB == t // HG_SUB))
    if rev:
        mats = [m[::-1, ::-1] for m in mats]
    full = np.concatenate(mats, axis=0).astype(np.float32)
    return np.concatenate([full, full, full], axis=1)


def _hgrn_kernel(layer, depth, nc_ctx, za_ref, lb_ref, g_ref, pmf_ref, pmb_ref, y_ref, of_scr, st_scr):
    c = HG_CHUNK
    u = HG_SUB
    nsub = c // u
    w4 = GROUP_W
    levels = _hgrn_levels()
    nc_tot = za_ref.shape[1] // c

    lbl = [lb_ref[j] for j in range(depth)]
    mx = functools.reduce(jnp.maximum, lbl)
    ex = [jnp.exp(v - mx) for v in lbl]
    den = functools.reduce(lambda a, b: a + b, ex)
    lower = jnp.zeros_like(den)
    for j in range(1, layer + 1):
        lower = lower + ex[j] / den

    r256 = lax.broadcasted_iota(jnp.int32, (w4, w4), 0)
    c256 = lax.broadcasted_iota(jnp.int32, (w4, w4), 1)
    headmask = (r256 // HEAD_DIM) == (c256 // HEAD_DIM)
    ones_bd = jnp.where(headmask, 1.0, 0.0).astype(BF16)
    trow = lax.broadcasted_iota(jnp.int32, (c, w4), 0)
    scol = lax.broadcasted_iota(jnp.int32, (c, w4), 1) % HEAD_DIM
    sub_r = lax.broadcasted_iota(jnp.int32, (nsub, u, w4), 1)

    def chunk(r0, rev):
        rows = pl.ds(r0, c)
        zq = za_ref[0, rows, 0:w4]
        v = za_ref[0, rows, w4:2 * w4]
        zf = za_ref[0, rows, (3 if rev else 2) * w4:(4 if rev else 3) * w4]
        lb = lower[1:2] if rev else lower[0:1]
        q = _silu(zq)
        f = lb + (1.0 - lb) * jax.nn.sigmoid(zf)
        k = 1.0 - f
        logf = jnp.log(f)

        hi = logf.astype(BF16)
        r1 = logf - hi.astype(F32)
        mid = r1.astype(BF16)
        lo = (r1 - mid.astype(F32)).astype(BF16)
        pm = pmb_ref[...] if rev else pmf_ref[...]
        esum = _dot(pm, jnp.concatenate([hi, mid, lo], axis=0))
        e_cum = esum[0:c]
        e_after = esum[c:2 * c]
        e_loc = esum[(2 + len(levels)) * c:(3 + len(levels)) * c]
        e_tot = e_cum[0:1] if rev else e_cum[c - 1:c]

        st = st_scr[...]
        vb = v.astype(BF16)
        o = _dot((q * jnp.exp(e_cum)).astype(BF16), st.astype(BF16), NT)

        att = jnp.zeros((c, w4), F32)
        for li, w in enumerate(levels):
            ew = jnp.exp(esum[(2 + li) * c:(3 + li) * c])
            is_q = ((trow // w) % 2 == 0) if rev else ((trow // w) % 2 == 1)
            qw = jnp.where(is_q, q * ew, 0.0).astype(BF16)
            kw = jnp.where(is_q, 0.0, k * ew).astype(BF16)
            kbd = jnp.where(headmask, jnp.concatenate([kw] * 4, axis=0), jnp.zeros((), BF16))
            a = _dot(qw, kbd, NT)
            att = att + jnp.where((trow // (2 * w)) == (scol // (2 * w)), a, 0.0)
        vbd = jnp.where(headmask, jnp.concatenate([vb] * 4, axis=0), jnp.zeros((), BF16))
        o = o + _dot(att.astype(BF16), vbd)

        c3 = e_loc.reshape(nsub, u, w4)
        q3 = q.reshape(nsub, u, w4)
        k3 = k.reshape(nsub, u, w4)
        v3 = v.reshape(nsub, u, w4)
        tiles = []
        for s in range(u):
            d = c3 - c3[:, s:s + 1, :]
            ok = (sub_r <= s) if rev else (sub_r >= s)
            wgt = jnp.where(ok, jnp.exp(jnp.minimum(d, 0.0)), 0.0)
            tiles.append((q3 * k3[:, s:s + 1, :] * wgt).reshape(c, w4))
        red = _seg_sum(jnp.concatenate(tiles, axis=0), ones_bd)
        od = jnp.zeros((nsub, u, w4), F32)
        for s in range(u):
            od = od + red[s * c:(s + 1) * c].reshape(nsub, u, w4) * v3[:, s:s + 1, :]
        o = o + od.reshape(c, w4)

        upd = _dot(vb, (k * jnp.exp(e_after)).astype(BF16), TN)
        st_scr[...] = st * jnp.exp(e_tot) + jnp.where(headmask, upd, 0.0)
        return o

    st_scr[...] = jnp.zeros_like(st_scr)

    def fwd_body(i, carry):
        r0 = pl.multiple_of(i * c, c)
        of_scr[pl.ds(r0, c), :] = chunk(r0, False)
        return carry

    lax.fori_loop(0, nc_tot, fwd_body, 0)

    st_scr[...] = jnp.zeros_like(st_scr)
    g = g_ref[...]

    def bwd_body(i, carry):
        ci = jnp.where(i < nc_ctx, nc_ctx - 1 - i, nc_tot - 1 - (i - nc_ctx))
        r0 = pl.multiple_of(ci * c, c)
        o = chunk(r0, True) + of_scr[pl.ds(r0, c), :]
        ms = _seg_sum(o * o, ones_bd) * (1.0 / HEAD_DIM)
        y_ref[0, pl.ds(r0, c), :] = o * lax.rsqrt(ms + RMS_EPS) * g
        return carry

    lax.fori_loop(0, nc_tot, bwd_body, 0)


def _hgrn_call(za, lb_logits, g4, pmf, pmb, layer, lc):
    bsz, t_tot, _ = za.shape
    depth = lb_logits.shape[0]
    return pl.pallas_call(
        functools.partial(_hgrn_kernel, layer, depth, lc // HG_CHUNK),
        grid=(bsz,),
        in_specs=[
            pl.BlockSpec((1, t_tot, A_W), lambda b: (b, 0, 0)),
            pl.BlockSpec((depth, 2, GROUP_W), lambda b: (0, 0, 0)),
            pl.BlockSpec((1, GROUP_W), lambda b: (0, 0)),
            pl.BlockSpec(pmf.shape, lambda b: (0, 0)),
            pl.BlockSpec(pmb.shape, lambda b: (0, 0)),
        ],
        out_specs=pl.BlockSpec((1, t_tot, GROUP_W), lambda b: (b, 0, 0)),
        out_shape=jax.ShapeDtypeStruct((bsz, t_tot, GROUP_W), F32),
        scratch_shapes=[pltpu.VMEM((t_tot, GROUP_W), F32), pltpu.VMEM((GROUP_W, GROUP_W), F32)],
        compiler_params=pltpu.CompilerParams(dimension_semantics=("arbitrary",),
                                             vmem_limit_bytes=48 * 1024 * 1024),
        name="hgrn2",
    )(za, lb_logits, g4, pmf, pmb)


def _lane_iota(rows):
    return lax.broadcasted_iota(jnp.int32, (rows, LANES), 1)


def _mask_q(q, lane, off, width):
    return jnp.where((lane >= off) & (lane < off + width), q, jnp.zeros((), q.dtype))


def _attn_b_kernel(tile0, nct, q_ref, k_ref, v_ref, o_ref):
    t = pl.program_id(1) + tile0
    lane = _lane_iota(TM)
    t_tot = k_ref.shape[1]

    def run(nk):
        k = k_ref[0, 0:nk, :]
        v = v_ref[0, 0:nk, :]
        for j in range(2):
            q = q_ref[0, :, j * LANES:(j + 1) * LANES]
            outs = []
            for half in range(2):
                lg = _dot(_mask_q(q, lane, half * HEAD_DIM, HEAD_DIM), k, NT)
                e = jnp.exp(lg - jnp.max(lg, axis=-1, keepdims=True))
                s = jnp.sum(e, axis=-1, keepdims=True)
                outs.append(_dot(e.astype(BF16), v) / s)
            o_ref[0, :, j * LANES:(j + 1) * LANES] = jnp.where(lane < HEAD_DIM, outs[0], outs[1])

    if tile0 < nct:
        @pl.when(t < nct)
        def _():
            run(nct * TM)

    @pl.when(t >= nct)
    def _():
        run(t_tot)


def _attn_c_kernel(tile0, nct, lam_init, q_ref, k_ref, v_ref, lam_ref, g_ref, o_ref):
    t = pl.program_id(1) + tile0
    lane = _lane_iota(TM)
    t_tot = k_ref.shape[1]
    lp = lam_ref[...]
    lam = (jnp.exp(jnp.sum(lp[0:1] * lp[1:2], axis=-1, keepdims=True))
           - jnp.exp(jnp.sum(lp[2:3] * lp[3:4], axis=-1, keepdims=True)) + lam_init)
    ones64 = _block_ones(LANES, HEAD_DIM)

    def run(nk):
        for hp in range(2):
            v = v_ref[0, 0:nk, hp * LANES:(hp + 1) * LANES]
            outs = []
            for hh in range(2):
                parts = []
                for m in range(2):
                    grp = (2 * hp + hh) * 2 + m
                    blk, off = grp // 4, (grp % 4) * DF_QK
                    q = q_ref[0, :, blk * LANES:(blk + 1) * LANES]
                    k = k_ref[0, 0:nk, blk * LANES:(blk + 1) * LANES]
                    lg = _dot(_mask_q(q, lane, off, DF_QK), k, NT)
                    e = jnp.exp(lg - jnp.max(lg, axis=-1, keepdims=True))
                    parts.append((e, jnp.sum(e, axis=-1, keepdims=True)))
                wgt = parts[0][0] * (1.0 / parts[0][1]) - parts[1][0] * (lam / parts[1][1])
                outs.append(_dot(wgt.astype(BF16), v))
            o = jnp.where(lane < HEAD_DIM, outs[0], outs[1])
            ms = _seg_sum(o * o, ones64) * (1.0 / HEAD_DIM)
            o_ref[0, :, hp * LANES:(hp + 1) * LANES] = o * lax.rsqrt(ms + RMS_EPS) * g_ref[...] * (1.0 - lam_init)

    if tile0 < nct:
        @pl.when(t < nct)
        def _():
            run(nct * TM)

    @pl.when(t >= nct)
    def _():
        run(t_tot)


def _attn_d_kernel(tile0, nct, sink_ref, q_ref, k_ref, v_ref, o_ref):
    t = pl.program_id(1) + tile0
    lane = _lane_iota(TM)
    lc = nct * TM
    s_lat = k_ref.shape[1] - lc
    win = TM + 2 * WINDOW

    def head_sink(j, half):
        return sink_ref[HEAD_PERM[2 * j + half]]

    def ctx_tile():
        k = k_ref[0, 0:lc, :]
        v = v_ref[0, 0:lc, :]
        for j in range(2):
            q = q_ref[0, :, j * LANES:(j + 1) * LANES]
            outs = []
            for half in range(2):
                sink = head_sink(j, half)
                lg = _dot(_mask_q(q, lane, half * HEAD_DIM, HEAD_DIM), k, NT)
                mx = jnp.maximum(jnp.max(lg, axis=-1, keepdims=True), sink)
                e = jnp.exp(lg - mx)
                s = jnp.sum(e, axis=-1, keepdims=True) + jnp.exp(sink - mx)
                outs.append(_dot(e.astype(BF16), v) / s)
            o_ref[0, :, j * LANES:(j + 1) * LANES] = jnp.where(lane < HEAD_DIM, outs[0], outs[1])

    def lat_tile():
        base = (t - nct) * TM
        wstart = pl.multiple_of(jnp.clip(base - WINDOW, 0, s_lat - win), WINDOW)
        kc = k_ref[0, 0:lc, :]
        vc = v_ref[0, 0:lc, :]
        kw = k_ref[0, pl.ds(lc + wstart, win), :]
        vw = v_ref[0, pl.ds(lc + wstart, win), :]
        qpos = base + lax.broadcasted_iota(jnp.int32, (TM, win), 0)
        kpos = wstart + lax.broadcasted_iota(jnp.int32, (TM, win), 1)
        allowed = jnp.abs(kpos - qpos) <= WINDOW
        for j in range(2):
            q = q_ref[0, :, j * LANES:(j + 1) * LANES]
            outs = []
            for half in range(2):
                sink = head_sink(j, half)
                qm = _mask_q(q, lane, half * HEAD_DIM, HEAD_DIM)
                lw = jnp.where(allowed, _dot(qm, kw, NT), -jnp.inf)
                lcx = _dot(qm, kc, NT)
                mx = jnp.maximum(jnp.maximum(jnp.max(lw, axis=-1, keepdims=True),
                                             jnp.max(lcx, axis=-1, keepdims=True)), sink)
                ew = jnp.exp(lw - mx)
                ec = jnp.exp(lcx - mx)
                s = (jnp.sum(ew, axis=-1, keepdims=True) + jnp.sum(ec, axis=-1, keepdims=True)
                     + jnp.exp(sink - mx))
                outs.append((_dot(ew.astype(BF16), vw) + _dot(ec.astype(BF16), vc)) / s)
            o_ref[0, :, j * LANES:(j + 1) * LANES] = jnp.where(lane < HEAD_DIM, outs[0], outs[1])

    if tile0 < nct:
        @pl.when(t < nct)
        def _():
            ctx_tile()

    @pl.when(t >= nct)
    def _():
        lat_tile()


def _attn_call(kind, at, lc, ctx_out, extra=(), lam_init=None):
    bsz, t_tot, _ = at.shape
    nct = lc // TM
    ntile = t_tot // TM
    tile0 = 0 if ctx_out else nct
    qb, kb, vb, nkv = {"B": (B_Q, B_K, B_V, 1), "C": (C_Q, C_K, C_V, 2), "D": (D_Q, D_K, D_V, 1)}[kind]
    q_spec = pl.BlockSpec((1, TM, 2 * LANES), lambda b, t: (b, t + tile0, qb // 2))
    k_spec = pl.BlockSpec((1, t_tot, nkv * LANES), lambda b, t: (b, 0, kb // nkv))
    v_spec = pl.BlockSpec((1, t_tot, nkv * LANES), lambda b, t: (b, 0, vb // nkv))
    o_spec = pl.BlockSpec((1, TM, GROUP_W), lambda b, t: (b, t, 0))
    if kind == "B":
        body = functools.partial(_attn_b_kernel, tile0, nct)
        in_specs, args = [q_spec, k_spec, v_spec], (at, at, at)
    elif kind == "C":
        body = functools.partial(_attn_c_kernel, tile0, nct, lam_init)
        lam, g2 = extra
        in_specs = [q_spec, k_spec, v_spec,
                    pl.BlockSpec(lam.shape, lambda b, t: (0, 0)), pl.BlockSpec((1, LANES), lambda b, t: (0, 0))]
        args = (at, at, at, lam, g2)
    else:
        body = functools.partial(_attn_d_kernel, tile0, nct)
        (sink,) = extra
        in_specs = [pl.BlockSpec(memory_space=pltpu.SMEM), q_spec, k_spec, v_spec]
        args = (sink, at, at, at)
    return pl.pallas_call(
        body,
        grid=(bsz, ntile - tile0),
        in_specs=in_specs,
        out_specs=o_spec,
        out_shape=jax.ShapeDtypeStruct((bsz, t_tot - tile0 * TM, GROUP_W), F32),
        compiler_params=pltpu.CompilerParams(dimension_semantics=("arbitrary", "arbitrary"),
                                             vmem_limit_bytes=48 * 1024 * 1024),
        name="attn_" + kind.lower(),
    )(*args)


def _outproj_kernel(nct, ctx_out, alpha, ya_ref, yb_ref, yc_ref, yd_ref, sg_ref, x_ref, c_ref, gt_ref,
                    w_ref, lg_ref, lb_ref, *out_refs):
    t = pl.program_id(1) + (0 if ctx_out else nct)
    y = jnp.concatenate([ya_ref[0], yb_ref[0], yc_ref[0], yd_ref[0]], axis=-1) * sg_ref[0]
    p = _dot(y.astype(BF16), w_ref[...])
    gate = gt_ref[0]

    def fin(res):
        z = alpha * res + gate * p
        mu = jnp.mean(z, axis=-1, keepdims=True)
        zc = z - mu
        var = jnp.mean(zc * zc, axis=-1, keepdims=True)
        return zc * lax.rsqrt(var + LN_EPS) * lg_ref[...] + lb_ref[...]

    if ctx_out:
        @pl.when(t < nct)
        def _():
            out_refs[1][0] = fin(c_ref[0])

    @pl.when(t >= nct)
    def _():
        out_refs[0][0] = fin(x_ref[0])


def _outproj_call(ys, sg, x, cx, modr, layer, wo, lng, lnb, ctx_out, alpha):
    bsz, s, _ = x.shape
    lc = cx.shape[1]
    nct, nlt = lc // TM, s // TM
    tile0 = 0 if ctx_out else nct

    def y_spec(arr):
        off = tile0 if arr.shape[1] == lc + s else 0
        return pl.BlockSpec((1, TM, GROUP_W), lambda b, t: (b, t + off, 0))

    vec = pl.BlockSpec((1, D_MODEL), lambda b, t: (0, 0))
    in_specs = [y_spec(a) for a in ys] + [
        pl.BlockSpec((1, TM, D_MODEL), lambda b, t: (b, t + tile0, 0)),
        pl.BlockSpec((1, TM, D_MODEL), lambda b, t: (b, jnp.maximum(t + tile0 - nct, 0), 0)),
        pl.BlockSpec((1, TM, D_MODEL), lambda b, t: (b, jnp.minimum(t + tile0, nct - 1), 0)),
        pl.BlockSpec((1, 1, D_MODEL),
                     lambda b, t: (layer * MOD_ROWS + jnp.where(t + tile0 < nct, MOD_ROWS - 8, b), 0, 2)),
        pl.BlockSpec((D_MODEL, D_MODEL), lambda b, t: (0, 0)),
        vec, vec,
    ]
    out_specs = [pl.BlockSpec((1, TM, D_MODEL), lambda b, t: (b, jnp.maximum(t + tile0 - nct, 0), 0))]
    out_shape = [jax.ShapeDtypeStruct((bsz, s, D_MODEL), F32)]
    if ctx_out:
        out_specs.append(pl.BlockSpec((1, TM, D_MODEL), lambda b, t: (b, jnp.minimum(t, nct - 1), 0)))
        out_shape.append(jax.ShapeDtypeStruct((bsz, lc, D_MODEL), F32))
    res = pl.pallas_call(
        functools.partial(_outproj_kernel, nct, ctx_out, alpha),
        grid=(bsz, nct + nlt - tile0),
        in_specs=in_specs,
        out_specs=out_specs,
        out_shape=out_shape,
        compiler_params=pltpu.CompilerParams(dimension_semantics=("arbitrary", "arbitrary"),
                                             vmem_limit_bytes=48 * 1024 * 1024),
        name="out_proj",
    )(*ys, sg, x, cx, modr, wo, lng, lnb)
    return (res[0], res[1]) if ctx_out else (res[0], None)


def _in_perm():
    hp = np.array(HEAD_PERM)
    head_cols = (hp[:, None] * HEAD_DIM + np.arange(HEAD_DIM)[None, :]).reshape(-1)
    idx = np.arange(4 * GROUP_W + 2 * GROUP_W + 3 * GROUP_W + 2 * GROUP_W + D_MODEL)
    b0 = A_W
    d0 = A_W + 2 * GROUP_W + 3 * GROUP_W
    idx[b0:b0 + GROUP_W] = b0 + head_cols
    idx[d0:d0 + GROUP_W] = d0 + head_cols
    idx[GATE_OFF + GROUP_W:GATE_OFF + 2 * GROUP_W] = GATE_OFF + GROUP_W + head_cols
    idx[GATE_OFF + 3 * GROUP_W:GATE_OFF + 4 * GROUP_W] = GATE_OFF + 3 * GROUP_W + head_cols
    return idx


def _out_perm():
    hp = np.array(HEAD_PERM)
    head_rows = (hp[:, None] * HEAD_DIM + np.arange(HEAD_DIM)[None, :]).reshape(-1)
    idx = np.arange(D_MODEL)
    idx[GROUP_W:2 * GROUP_W] = GROUP_W + head_rows
    idx[3 * GROUP_W:4 * GROUP_W] = 3 * GROUP_W + head_rows
    return idx


def _rope_tables(n_lat, n_ctx, dim):
    rows = n_lat // GRID_W
    row = jnp.repeat(jnp.arange(rows, dtype=F32), GRID_W)
    col = jnp.broadcast_to(jnp.arange(GRID_W, dtype=F32), (rows, GRID_W)).reshape(-1)
    d_axis = dim // 2
    inv = ROPE_THETA ** (-jnp.arange(0, d_axis, 2, dtype=F32) / d_axis)
    ang_r = row[:, None] * inv
    ang_c = col[:, None] * inv
    ang = jnp.concatenate([ang_r, ang_r, ang_c, ang_c], axis=-1)
    cos, sin = jnp.cos(ang), jnp.sin(ang)
    first = (np.arange(dim) % (dim // 2)) < (dim // 4)
    sa = jnp.where(first[None, :], -sin, 0.0)
    sb = jnp.where(first[None, :], 0.0, sin)
    reps = LANES // dim
    pad = lambda a, fill: jnp.concatenate([jnp.full((n_ctx, LANES), fill, F32), jnp.tile(a, (1, reps))], axis=0)
    return pad(cos, 1.0), pad(sa, 0.0), pad(sb, 0.0)


def kernel(x, c, ctx, c_ctx, w_in, w_out, w_ada, b_ada, ln_g, ln_b, hg_lb_logits, hg_norm_g, ga_q_norm_g,
           ga_k_norm_g, df_lambda, df_subln_g, wn_sink):
    bsz, s, d = x.shape
    lc = ctx.shape[1]
    depth = w_in.shape[0]
    assert d == D_MODEL and s % TM == 0 and lc % TM == 0 and s >= TM + 2 * WINDOW and bsz <= MOD_ROWS - 8
    assert w_in.shape[2] == IN_W

    cc = jnp.concatenate([c, jnp.zeros((MOD_ROWS - 8 - bsz, d), F32), c_ctx[None, :],
                          jnp.zeros((7, d), F32)], axis=0)
    mod = _ada_call(cc, w_ada, b_ada)
    modr = mod.reshape(depth * MOD_ROWS, 1, 3 * d)

    tabs = _rope_tables(s, lc, HEAD_DIM) + _rope_tables(s, lc, DF_QK)
    wp = w_in[:, :, _in_perm()].astype(BF16)
    wo = w_out[:, _out_perm(), :].astype(BF16)
    pmf = jnp.asarray(_hgrn_sum_matrix(False), BF16)
    pmb = jnp.asarray(_hgrn_sum_matrix(True), BF16)
    tile2 = lambda v: jnp.tile(v, 2)[None, :]
    alpha = (2.0 * depth) ** 0.25

    cx = ctx
    for l in range(depth):
        ctx_out = l < depth - 1
        za, at, sg = _inproj_call(x, cx, modr, l, wp[l], tile2(ga_q_norm_g[l]), tile2(ga_k_norm_g[l]), tabs)
        ya = _hgrn_call(za, hg_lb_logits, jnp.tile(hg_norm_g[l], 4)[None, :], pmf, pmb, l, lc)
        lam_init = 0.8 - 0.6 * math.exp(-0.3 * l)
        yb = _attn_call("B", at, lc, ctx_out)
        yc = _attn_call("C", at, lc, ctx_out, extra=(df_lambda[l], tile2(df_subln_g[l])), lam_init=lam_init)
        yd = _attn_call("D", at, lc, ctx_out, extra=(wn_sink[l],))
        x, cx_new = _outproj_call((ya, yb, yc, yd), sg, x, cx, modr, l, wo[l],
                                  ln_g[l][None, :], ln_b[l][None, :], ctx_out, alpha)
        cx = cx_new if ctx_out else cx
    return x
```

```python
import functools
import math

import numpy as np
import jax
import jax.numpy as jnp
from jax import lax
from jax.experimental import pallas as pl
from jax.experimental.pallas import tpu as pltpu

F32 = jnp.float32
BF16 = jnp.bfloat16

D_MODEL = 1024
GROUP_W = 256
HEAD_DIM = 64
DF_QK = 32
GRID_W = 64
WINDOW = 128
ROPE_THETA = 10000.0
LN_EPS = 1e-5
RMS_EPS = 1e-6
LOG2E = 1.0 / math.log(2.0)
HG_CHUNK = 64
HG_SUB = 8
HG_UNROLL = 4

LANES = 128
TM = 256
MOD_ROWS = 24

A_W = 4 * GROUP_W
ATT_W = 14 * LANES
ATT_OFF = A_W
GATE_OFF = A_W + ATT_W
IN_W = GATE_OFF + D_MODEL
B_Q, B_K, B_V = 0, 2, 3
C_Q, C_K, C_V = 4, 6, 8
D_Q, D_K, D_V = 10, 12, 13
HEAD_PERM = (0, 2, 1, 3)

NT = (((1,), (1,)), ((), ()))
TN = (((0,), (0,)), ((), ()))


def _dot(a, b, dims=None):
    if dims is None:
        return jnp.dot(a, b, preferred_element_type=F32)
    return lax.dot_general(a, b, dims, preferred_element_type=F32)


def _split2(x):
    hi = x.astype(BF16)
    lo = (x - hi.astype(F32)).astype(BF16)
    return hi, lo


def _seg_sum(x, ones_bd):
    hi, lo = _split2(x)
    return _dot(hi, ones_bd) + _dot(lo, ones_bd)


def _block_ones(n, seg):
    r = lax.broadcasted_iota(jnp.int32, (n, n), 0) // seg
    c = lax.broadcasted_iota(jnp.int32, (n, n), 1) // seg
    return jnp.where(r == c, 1.0, 0.0).astype(BF16)


def _silu(x):
    return x * jax.nn.sigmoid(x)


def _ada_kernel(c_ref, w_ref, b_ref, o_ref):
    s = _silu(c_ref[...])
    o_ref[0] = jnp.dot(s, w_ref[0], preferred_element_type=F32, precision=lax.Precision.HIGHEST) + b_ref[0]


def _ada_call(cc, w_ada, b_ada):
    depth = w_ada.shape[0]
    n3 = w_ada.shape[2]
    nb = n3 // D_MODEL
    return pl.pallas_call(
        _ada_kernel,
        grid=(depth, nb),
        in_specs=[
            pl.BlockSpec((MOD_ROWS, D_MODEL), lambda l, j: (0, 0)),
            pl.BlockSpec((1, D_MODEL, D_MODEL), lambda l, j: (l, 0, j)),
            pl.BlockSpec((1, 1, D_MODEL), lambda l, j: (l, 0, j)),
        ],
        out_specs=pl.BlockSpec((1, MOD_ROWS, D_MODEL), lambda l, j: (l, 0, j)),
        out_shape=jax.ShapeDtypeStruct((depth, MOD_ROWS, n3), F32),
        compiler_params=pltpu.CompilerParams(dimension_semantics=("arbitrary", "arbitrary"),
                                             vmem_limit_bytes=32 * 1024 * 1024),
        name="ada_mod",
    )(cc, w_ada, b_ada.reshape(depth, 1, n3))


def _rope(x, cos, sa, sb, shift):
    up = pltpu.roll(x, LANES - shift, axis=1)
    dn = pltpu.roll(x, shift, axis=1)
    return x * cos + up * sa + dn * sb


def _inproj_kernel(nct, x_ref, c_ref, sh_ref, sc_ref, w_ref, qn_ref, kn_ref,
                   c64_ref, sa64_ref, sb64_ref, c32_ref, sa32_ref, sb32_ref,
                   za_ref, at_ref, sg_ref, h_scr):
    t = pl.program_id(1)
    sc = 1.0 + sc_ref[0]
    sh = sh_ref[0]

    @pl.when(t < nct)
    def _():
        h_scr[...] = (c_ref[0] * sc + sh).astype(BF16)

    @pl.when(t >= nct)
    def _():
        h_scr[...] = (x_ref[0] * sc + sh).astype(BF16)

    h = h_scr[...]

    def proj(blk0, nblk):
        c0 = ATT_OFF + blk0 * LANES
        return _dot(h, w_ref[:, c0:c0 + nblk * LANES])

    def put(blk, val):
        at_ref[0, :, blk * LANES:(blk + 1) * LANES] = val.astype(BF16)

    def lanes(a, j):
        return a[:, j * LANES:(j + 1) * LANES]

    acc = _dot(h, w_ref[:, 0:A_W])
    for j in range(A_W // LANES):
        za_ref[0, j] = lanes(acc, j)

    ones64 = _block_ones(LANES, HEAD_DIM)
    c64, sa64, sb64 = c64_ref[...], sa64_ref[...], sb64_ref[...]
    c32, sa32, sb32 = c32_ref[...], sa32_ref[...], sb32_ref[...]
    q64 = HEAD_DIM // 4
    q32 = DF_QK // 4

    def rms(v, g):
        ms = _seg_sum(v * v, ones64) * (1.0 / HEAD_DIM)
        return v * lax.rsqrt(ms + RMS_EPS) * g

    acc = proj(B_Q, 4)
    for j in range(2):
        put(B_Q + j, _rope(rms(lanes(acc, j), qn_ref[...]), c64, sa64, sb64, q64) * (HEAD_DIM ** -0.5 * LOG2E))
    put(B_K, _rope(rms(lanes(acc, 2), kn_ref[...]), c64, sa64, sb64, q64))
    put(B_V, lanes(acc, 3))

    acc = proj(C_Q, 6)
    for j in range(2):
        put(C_Q + j, _rope(lanes(acc, j), c32, sa32, sb32, q32) * (DF_QK ** -0.5 * LOG2E))
        put(C_K + j, _rope(lanes(acc, 2 + j), c32, sa32, sb32, q32))
        put(C_V + j, lanes(acc, 4 + j))

    acc = proj(D_Q, 4)
    for j in range(2):
        put(D_Q + j, _rope(lanes(acc, j), c64, sa64, sb64, q64) * (HEAD_DIM ** -0.5))
    put(D_K, _rope(lanes(acc, 2), c64, sa64, sb64, q64))
    put(D_V, lanes(acc, 3))

    sg_ref[0] = _silu(_dot(h, w_ref[:, GATE_OFF:IN_W]))


def _inproj_call(x, cx, modr, layer, wp, qn, kn, tabs):
    bsz, s, _ = x.shape
    lc = cx.shape[1]
    nct, nlt = lc // TM, s // TM
    t_tot = lc + s
    tab_spec = pl.BlockSpec((TM, LANES), lambda b, t: (t, 0))
    vec_spec = pl.BlockSpec((1, LANES), lambda b, t: (0, 0))

    def mod_spec(col):
        return pl.BlockSpec((1, 1, D_MODEL),
                            lambda b, t: (layer * MOD_ROWS + jnp.where(t < nct, MOD_ROWS - 8, b), 0, col))

    return pl.pallas_call(
        functools.partial(_inproj_kernel, nct),
        grid=(bsz, nct + nlt),
        in_specs=[
            pl.BlockSpec((1, TM, D_MODEL), lambda b, t: (b, jnp.maximum(t - nct, 0), 0)),
            pl.BlockSpec((1, TM, D_MODEL), lambda b, t: (b, jnp.minimum(t, nct - 1), 0)),
            mod_spec(0), mod_spec(1),
            pl.BlockSpec((D_MODEL, IN_W), lambda b, t: (0, 0)),
            vec_spec, vec_spec,
            tab_spec, tab_spec, tab_spec, tab_spec, tab_spec, tab_spec,
        ],
        out_specs=[
            pl.BlockSpec((1, A_W // LANES, TM, LANES), lambda b, t: (b, 0, t, 0)),
            pl.BlockSpec((1, TM, ATT_W), lambda b, t: (b, t, 0)),
            pl.BlockSpec((1, TM, D_MODEL), lambda b, t: (b, t, 0)),
        ],
        out_shape=[
            jax.ShapeDtypeStruct((bsz, A_W // LANES, t_tot, LANES), F32),
            jax.ShapeDtypeStruct((bsz, t_tot, ATT_W), BF16),
            jax.ShapeDtypeStruct((bsz, t_tot, D_MODEL), F32),
        ],
        scratch_shapes=[pltpu.VMEM((TM, D_MODEL), BF16)],
        compiler_params=pltpu.CompilerParams(dimension_semantics=("arbitrary", "arbitrary"),
                                             vmem_limit_bytes=48 * 1024 * 1024),
        name="in_proj",
    )(x, cx, modr, modr, wp, qn, kn, *tabs)


def _hgrn_levels():
    w = HG_CHUNK // 2
    out = []
    while w >= HG_SUB:
        out.append(w)
        w //= 2
    return tuple(out)


def _hgrn_time_of_row():
    nsub = HG_CHUNK // HG_SUB
    r = np.arange(HG_CHUNK)
    return (r % nsub) * HG_SUB + r // nsub


def _hgrn_sum_matrix(rev):
    c = HG_CHUNK
    t = np.arange(c)[:, None]
    j = np.arange(c)[None, :]
    mats = [j <= t, j > t]
    for w in _hgrn_levels():
        m = (t // (2 * w)) * 2 * w + w
        mats.append(np.where(t >= m, (j >= m) & (j <= t), (j > t) & (j <= m - 1)))
    if rev:
        mats = [m[::-1, ::-1] for m in mats]
    tor = _hgrn_time_of_row()
    full = np.concatenate([m[tor][:, tor] for m in mats], axis=0).astype(np.float32)
    return np.concatenate([full, full, full], axis=1)


def _hgrn_kernel(layer, depth, nc_ctx, za_ref, lb_ref, g_ref, pmf_ref, pmb_ref, y_ref,
                 of_scr, ob_scr, stf_scr, stb_scr):
    c = HG_CHUNK
    u = HG_SUB
    nsub = c // u
    w4 = GROUP_W
    levels = _hgrn_levels()
    t_tot = za_ref.shape[2]
    nc_tot = t_tot // c
    log2e = 1.0 / math.log(2.0)

    lbl = [lb_ref[j] for j in range(depth)]
    mx = functools.reduce(jnp.maximum, lbl)
    ex = [jnp.exp(v - mx) for v in lbl]
    den = functools.reduce(lambda a, b: a + b, ex)
    lower = jnp.zeros_like(den)
    for j in range(1, layer + 1):
        lower = lower + ex[j] / den

    r256 = lax.broadcasted_iota(jnp.int32, (w4, w4), 0)
    c256 = lax.broadcasted_iota(jnp.int32, (w4, w4), 1)
    headmask = (r256 // HEAD_DIM) == (c256 // HEAD_DIM)
    ones_bd = jnp.where(headmask, 1.0, 0.0).astype(BF16)
    rrow = lax.broadcasted_iota(jnp.int32, (c, w4), 0)
    trow = (rrow % nsub) * u + rrow // nsub
    rcol = lax.broadcasted_iota(jnp.int32, (c, w4), 1) % HEAD_DIM
    scol = (rcol % nsub) * u + rcol // nsub

    def load(r0, slab0):
        return jnp.concatenate(
            [jnp.concatenate([za_ref[0, slab0 + j, pl.ds(r0 + p, nsub, stride=u), :] for p in range(u)], axis=0)
             for j in range(2)], axis=1)

    def chunk(r0, rev, st_scr):
        zq = load(r0, 0)
        v = load(r0, 2)
        zf = load(r0, 6 if rev else 4)
        lb = lower[1:2] if rev else lower[0:1]
        q = _silu(zq)
        f = lb + (1.0 - lb) * jax.nn.sigmoid(zf)
        k = 1.0 - f
        lf2 = jnp.log(f) * log2e

        hi = lf2.astype(BF16)
        r1 = lf2 - hi.astype(F32)
        mid = r1.astype(BF16)
        lo = (r1 - mid.astype(F32)).astype(BF16)
        pm = pmb_ref[...] if rev else pmf_ref[...]
        esum = _dot(pm, jnp.concatenate([hi, mid, lo], axis=0))
        e_cum = esum[0:c]
        e_after = esum[c:2 * c]
        e_tot = e_cum[0:1] if rev else e_cum[c - 1:c]

        st = st_scr[...]
        vb = v.astype(BF16)
        o = _dot((q * jnp.exp2(e_cum)).astype(BF16), st.astype(BF16), NT)

        att = jnp.zeros((c, w4), F32)
        for li, w in enumerate(levels):
            ew = jnp.exp2(esum[(2 + li) * c:(3 + li) * c])
            is_q = ((trow // w) % 2 == 0) if rev else ((trow // w) % 2 == 1)
            qw = jnp.where(is_q, q * ew, 0.0).astype(BF16)
            kw = jnp.where(is_q, 0.0, k * ew).astype(BF16)
            kbd = jnp.where(headmask, jnp.concatenate([kw] * 4, axis=0), jnp.zeros((), BF16))
            a = _dot(qw, kbd, NT)
            att = att + jnp.where((trow // (2 * w)) == (scol // (2 * w)), a, 0.0)
        vbd = jnp.where(headmask, jnp.concatenate([vb] * 4, axis=0), jnp.zeros((), BF16))
        o = o + _dot(att.astype(BF16), vbd)

        grp = lambda a, p: a[p * nsub:(p + 1) * nsub]
        pairs = [(p, s) for p in range(u) for s in range(u) if (s >= p if rev else s <= p)]
        tiles = []
        for p, s in pairs:
            qk = grp(q, p) * grp(k, s)
            tiles.append(qk if p == s else qk * jnp.exp2(grp(e_cum, p) - grp(e_cum, s)))
        red = _dot(jnp.concatenate(tiles, axis=0).astype(BF16), ones_bd)
        ods = [None] * u
        for i, (p, s) in enumerate(pairs):
            term = red[i * nsub:(i + 1) * nsub] * grp(v, s)
            ods[p] = term if ods[p] is None else ods[p] + term
        o = o + jnp.concatenate(ods, axis=0)

        upd = _dot(vb, (k * jnp.exp2(e_after)).astype(BF16), TN)
        st_scr[...] = st * jnp.exp2(e_tot) + jnp.where(headmask, upd, 0.0)
        return o

    def store(o_scr, r0, o):
        for j in range(2):
            for p in range(u):
                o_scr[j, pl.ds(r0 + p, nsub, stride=u), :] = o[p * nsub:(p + 1) * nsub, j * LANES:(j + 1) * LANES]

    stf_scr[...] = jnp.zeros_like(stf_scr)
    stb_scr[...] = jnp.zeros_like(stb_scr)

    def body(i2, carry):
        for jj in range(HG_UNROLL):
            i = i2 * HG_UNROLL + jj
            r0 = pl.multiple_of(i * c, c)
            store(of_scr, r0, chunk(r0, False, stf_scr))
            ci = jnp.where(i < nc_ctx, nc_ctx - 1 - i, nc_tot - 1 - (i - nc_ctx))
            r1 = pl.multiple_of(ci * c, c)
            store(ob_scr, r1, chunk(r1, True, stb_scr))
        return carry

    lax.fori_loop(0, nc_tot // HG_UNROLL, body, 0)

    g = g_ref[...]
    nrow = 4 * c

    def norm_body(i, carry):
        r0 = pl.multiple_of(i * nrow, nrow)
        o = jnp.concatenate([of_scr[j, pl.ds(r0, nrow), :] + ob_scr[j, pl.ds(r0, nrow), :] for j in range(2)], axis=1)
        ms = _dot((o * o).astype(BF16), ones_bd) * (1.0 / HEAD_DIM)
        y_ref[0, pl.ds(r0, nrow), :] = o * lax.rsqrt(ms + RMS_EPS) * g
        return carry

    lax.fori_loop(0, t_tot // nrow, norm_body, 0)


def _hgrn_call(za, lb_logits, g4, pmf, pmb, layer, lc):
    bsz, nslab, t_tot, _ = za.shape
    depth = lb_logits.shape[0]
    return pl.pallas_call(
        functools.partial(_hgrn_kernel, layer, depth, lc // HG_CHUNK),
        grid=(bsz,),
        in_specs=[
            pl.BlockSpec((1, nslab, t_tot, LANES), lambda b: (b, 0, 0, 0)),
            pl.BlockSpec((depth, 2, GROUP_W), lambda b: (0, 0, 0)),
            pl.BlockSpec((1, GROUP_W), lambda b: (0, 0)),
            pl.BlockSpec(pmf.shape, lambda b: (0, 0)),
            pl.BlockSpec(pmb.shape, lambda b: (0, 0)),
        ],
        out_specs=pl.BlockSpec((1, t_tot, GROUP_W), lambda b: (b, 0, 0)),
        out_shape=jax.ShapeDtypeStruct((bsz, t_tot, GROUP_W), F32),
        scratch_shapes=[pltpu.VMEM((2, t_tot, LANES), F32), pltpu.VMEM((2, t_tot, LANES), F32),
                        pltpu.VMEM((GROUP_W, GROUP_W), F32), pltpu.VMEM((GROUP_W, GROUP_W), F32)],
        compiler_params=pltpu.CompilerParams(dimension_semantics=("arbitrary",),
                                             vmem_limit_bytes=48 * 1024 * 1024),
        name="hgrn2",
    )(za, lb_logits, g4, pmf, pmb)


def _lane_iota(rows):
    return lax.broadcasted_iota(jnp.int32, (rows, LANES), 1)


def _mask_q(q, lane, off, width):
    return jnp.where((lane >= off) & (lane < off + width), q, jnp.zeros((), q.dtype))


def _fill_v_ones(vx_scr, v_ref):
    @pl.when(pl.program_id(1) == 0)
    def _():
        vx_scr[:, 0:LANES] = v_ref[0]
        vx_scr[:, LANES:2 * LANES] = jnp.ones((v_ref.shape[1], LANES), BF16)


def _attn_b_kernel(tile0, nct, q_ref, k_ref, v_ref, o_ref, vx_scr, l_scr, e_scr):
    t = pl.program_id(1) + tile0
    lane = _lane_iota(TM)
    t_tot = k_ref.shape[1]
    _fill_v_ones(vx_scr, v_ref)

    def run(nk):
        def logits(hd):
            j, half = divmod(hd, 2)
            q = q_ref[0, :, j * LANES:(j + 1) * LANES]
            l_scr[hd, :, 0:nk] = _dot(_mask_q(q, lane, half * HEAD_DIM, HEAD_DIM), k_ref[0, 0:nk, :], NT)

        def softmax(hd):
            mx = jnp.max(l_scr[hd, :, 0:nk], axis=-1, keepdims=True)
            e_scr[hd, :, 0:nk] = jnp.exp2(l_scr[hd, :, 0:nk] - mx).astype(BF16)

        def values(hd):
            acc = _dot(e_scr[hd, :, 0:nk], vx_scr[0:nk, :])
            return acc[:, 0:LANES] / acc[:, LANES:2 * LANES]

        nh = 4
        outs = [None] * nh
        logits(0)
        for hd in range(nh):
            if hd + 1 < nh:
                logits(hd + 1)
            softmax(hd)
            outs[hd] = values(hd)
        for j in range(2):
            o_ref[0, :, j * LANES:(j + 1) * LANES] = jnp.where(lane < HEAD_DIM, outs[2 * j], outs[2 * j + 1])

    if tile0 < nct:
        @pl.when(t < nct)
        def _():
            run(nct * TM)

    @pl.when(t >= nct)
    def _():
        run(t_tot)


def _attn_c_kernel(tile0, nct, lam_init, q_ref, k_ref, v_ref, lam_ref, g_ref, o_ref, l_scr, e_scr):
    t = pl.program_id(1) + tile0
    lane = _lane_iota(TM)
    t_tot = k_ref.shape[1]
    lp = lam_ref[...]
    lam = (jnp.exp(jnp.sum(lp[0:1] * lp[1:2], axis=-1, keepdims=True))
           - jnp.exp(jnp.sum(lp[2:3] * lp[3:4], axis=-1, keepdims=True)) + lam_init)
    ones64 = _block_ones(LANES, HEAD_DIM)

    def run(nk):
        def logits(hd):
            for m in range(2):
                grp = 2 * hd + m
                blk, off = grp // 4, (grp % 4) * DF_QK
                q = q_ref[0, :, blk * LANES:(blk + 1) * LANES]
                l_scr[grp, :, 0:nk] = _dot(_mask_q(q, lane, off, DF_QK),
                                           k_ref[0, 0:nk, blk * LANES:(blk + 1) * LANES], NT)

        def softmax(hd):
            sums = []
            for m in range(2):
                grp = 2 * hd + m
                mx = jnp.max(l_scr[grp, :, 0:nk], axis=-1, keepdims=True)
                l_scr[grp, :, 0:nk] = jnp.exp2(l_scr[grp, :, 0:nk] - mx)
                sums.append(jnp.sum(l_scr[grp, :, 0:nk], axis=-1, keepdims=True))
            wgt = l_scr[2 * hd, :, 0:nk] * (1.0 / sums[0]) - l_scr[2 * hd + 1, :, 0:nk] * (lam / sums[1])
            e_scr[hd, :, 0:nk] = wgt.astype(BF16)

        def values(hd):
            hp = hd // 2
            return _dot(e_scr[hd, :, 0:nk], v_ref[0, 0:nk, hp * LANES:(hp + 1) * LANES])

        nh = 4
        outs = [None] * nh
        logits(0)
        for hd in range(nh):
            if hd + 1 < nh:
                logits(hd + 1)
            softmax(hd)
            outs[hd] = values(hd)
        for hp in range(2):
            o = jnp.where(lane < HEAD_DIM, outs[2 * hp], outs[2 * hp + 1])
            ms = _seg_sum(o * o, ones64) * (1.0 / HEAD_DIM)
            o_ref[0, :, hp * LANES:(hp + 1) * LANES] = o * lax.rsqrt(ms + RMS_EPS) * g_ref[...] * (1.0 - lam_init)

    if tile0 < nct:
        @pl.when(t < nct)
        def _():
            run(nct * TM)

    @pl.when(t >= nct)
    def _():
        run(t_tot)


def _attn_d_kernel(tile0, nct, sink_ref, q_ref, k_ref, v_ref, o_ref):
    t = pl.program_id(1) + tile0
    lane = _lane_iota(TM)
    lc = nct * TM
    s_lat = k_ref.shape[1] - lc
    win = TM + 2 * WINDOW

    def head_sink(j, half):
        return sink_ref[HEAD_PERM[2 * j + half]]

    def ctx_tile():
        k = k_ref[0, 0:lc, :]
        v = v_ref[0, 0:lc, :]
        for j in range(2):
            q = q_ref[0, :, j * LANES:(j + 1) * LANES]
            outs = []
            for half in range(2):
                sink = head_sink(j, half)
                lg = _dot(_mask_q(q, lane, half * HEAD_DIM, HEAD_DIM), k, NT)
                mx = jnp.maximum(jnp.max(lg, axis=-1, keepdims=True), sink)
                e = jnp.exp(lg - mx)
                s = jnp.sum(e, axis=-1, keepdims=True) + jnp.exp(sink - mx)
                outs.append(_dot(e.astype(BF16), v) / s)
            o_ref[0, :, j * LANES:(j + 1) * LANES] = jnp.where(lane < HEAD_DIM, outs[0], outs[1])

    def lat_tile():
        base = (t - nct) * TM
        wstart = pl.multiple_of(jnp.clip(base - WINDOW, 0, s_lat - win), WINDOW)
        kc = k_ref[0, 0:lc, :]
        vc = v_ref[0, 0:lc, :]
        kw = k_ref[0, pl.ds(lc + wstart, win), :]
        vw = v_ref[0, pl.ds(lc + wstart, win), :]
        qpos = base + lax.broadcasted_iota(jnp.int32, (TM, win), 0)
        kpos = wstart + lax.broadcasted_iota(jnp.int32, (TM, win), 1)
        allowed = jnp.abs(kpos - qpos) <= WINDOW
        for j in range(2):
            q = q_ref[0, :, j * LANES:(j + 1) * LANES]
            outs = []
            for half in range(2):
                sink = head_sink(j, half)
                qm = _mask_q(q, lane, half * HEAD_DIM, HEAD_DIM)
                lw = jnp.where(allowed, _dot(qm, kw, NT), -jnp.inf)
                lcx = _dot(qm, kc, NT)
                mx = jnp.maximum(jnp.maximum(jnp.max(lw, axis=-1, keepdims=True),
                                             jnp.max(lcx, axis=-1, keepdims=True)), sink)
                ew = jnp.exp(lw - mx)
                ec = jnp.exp(lcx - mx)
                s = (jnp.sum(ew, axis=-1, keepdims=True) + jnp.sum(ec, axis=-1, keepdims=True)
                     + jnp.exp(sink - mx))
                outs.append((_dot(ew.astype(BF16), vw) + _dot(ec.astype(BF16), vc)) / s)
            o_ref[0, :, j * LANES:(j + 1) * LANES] = jnp.where(lane < HEAD_DIM, outs[0], outs[1])

    if tile0 < nct:
        @pl.when(t < nct)
        def _():
            ctx_tile()

    @pl.when(t >= nct)
    def _():
        lat_tile()


def _attn_call(kind, at, lc, ctx_out, extra=(), lam_init=None):
    bsz, t_tot, _ = at.shape
    nct = lc // TM
    ntile = t_tot // TM
    tile0 = 0 if ctx_out else nct
    qb, kb, vb, nkv = {"B": (B_Q, B_K, B_V, 1), "C": (C_Q, C_K, C_V, 2), "D": (D_Q, D_K, D_V, 1)}[kind]
    q_spec = pl.BlockSpec((1, TM, 2 * LANES), lambda b, t: (b, t + tile0, qb // 2))
    k_spec = pl.BlockSpec((1, t_tot, nkv * LANES), lambda b, t: (b, 0, kb // nkv))
    v_spec = pl.BlockSpec((1, t_tot, nkv * LANES), lambda b, t: (b, 0, vb // nkv))
    o_spec = pl.BlockSpec((1, TM, GROUP_W), lambda b, t: (b, t, 0))
    scratch = []
    if kind == "B":
        body = functools.partial(_attn_b_kernel, tile0, nct)
        in_specs, args = [q_spec, k_spec, v_spec], (at, at, at)
        scratch = [pltpu.VMEM((t_tot, 2 * LANES), BF16), pltpu.VMEM((4, TM, t_tot), F32),
                   pltpu.VMEM((4, TM, t_tot), BF16)]
    elif kind == "C":
        body = functools.partial(_attn_c_kernel, tile0, nct, lam_init)
        lam, g2 = extra
        in_specs = [q_spec, k_spec, v_spec,
                    pl.BlockSpec(lam.shape, lambda b, t: (0, 0)), pl.BlockSpec((1, LANES), lambda b, t: (0, 0))]
        args = (at, at, at, lam, g2)
        scratch = [pltpu.VMEM((8, TM, t_tot), F32), pltpu.VMEM((4, TM, t_tot), BF16)]
    else:
        body = functools.partial(_attn_d_kernel, tile0, nct)
        (sink,) = extra
        in_specs = [pl.BlockSpec(memory_space=pltpu.SMEM), q_spec, k_spec, v_spec]
        args = (sink, at, at, at)
    return pl.pallas_call(
        body,
        grid=(bsz, ntile - tile0),
        in_specs=in_specs,
        out_specs=o_spec,
        out_shape=jax.ShapeDtypeStruct((bsz, t_tot - tile0 * TM, GROUP_W), F32),
        scratch_shapes=scratch,
        compiler_params=pltpu.CompilerParams(dimension_semantics=("arbitrary", "arbitrary"),
                                             vmem_limit_bytes=48 * 1024 * 1024),
        name="attn_" + kind.lower(),
    )(*args)


def _outproj_kernel(nct, ctx_out, alpha, ya_ref, yb_ref, yc_ref, yd_ref, sg_ref, x_ref, c_ref, gt_ref,
                    w_ref, lg_ref, lb_ref, *out_refs):
    t = pl.program_id(1) + (0 if ctx_out else nct)
    y = jnp.concatenate([ya_ref[0], yb_ref[0], yc_ref[0], yd_ref[0]], axis=-1) * sg_ref[0]
    p = _dot(y.astype(BF16), w_ref[...])
    gate = gt_ref[0]

    def fin(res):
        z = alpha * res + gate * p
        mu = jnp.mean(z, axis=-1, keepdims=True)
        zc = z - mu
        var = jnp.mean(zc * zc, axis=-1, keepdims=True)
        return zc * lax.rsqrt(var + LN_EPS) * lg_ref[...] + lb_ref[...]

    if ctx_out:
        @pl.when(t < nct)
        def _():
            out_refs[1][0] = fin(c_ref[0])

    @pl.when(t >= nct)
    def _():
        out_refs[0][0] = fin(x_ref[0])


def _outproj_call(ys, sg, x, cx, modr, layer, wo, lng, lnb, ctx_out, alpha):
    bsz, s, _ = x.shape
    lc = cx.shape[1]
    nct, nlt = lc // TM, s // TM
    tile0 = 0 if ctx_out else nct

    def y_spec(arr):
        off = tile0 if arr.shape[1] == lc + s else 0
        return pl.BlockSpec((1, TM, GROUP_W), lambda b, t: (b, t + off, 0))

    vec = pl.BlockSpec((1, D_MODEL), lambda b, t: (0, 0))
    in_specs = [y_spec(a) for a in ys] + [
        pl.BlockSpec((1, TM, D_MODEL), lambda b, t: (b, t + tile0, 0)),
        pl.BlockSpec((1, TM, D_MODEL), lambda b, t: (b, jnp.maximum(t + tile0 - nct, 0), 0)),
        pl.BlockSpec((1, TM, D_MODEL), lambda b, t: (b, jnp.minimum(t + tile0, nct - 1), 0)),
        pl.BlockSpec((1, 1, D_MODEL),
                     lambda b, t: (layer * MOD_ROWS + jnp.where(t + tile0 < nct, MOD_ROWS - 8, b), 0, 2)),
        pl.BlockSpec((D_MODEL, D_MODEL), lambda b, t: (0, 0)),
        vec, vec,
    ]
    out_specs = [pl.BlockSpec((1, TM, D_MODEL), lambda b, t: (b, jnp.maximum(t + tile0 - nct, 0), 0))]
    out_shape = [jax.ShapeDtypeStruct((bsz, s, D_MODEL), F32)]
    if ctx_out:
        out_specs.append(pl.BlockSpec((1, TM, D_MODEL), lambda b, t: (b, jnp.minimum(t, nct - 1), 0)))
        out_shape.append(jax.ShapeDtypeStruct((bsz, lc, D_MODEL), F32))
    res = pl.pallas_call(
        functools.partial(_outproj_kernel, nct, ctx_out, alpha),
        grid=(bsz, nct + nlt - tile0),
        in_specs=in_specs,
        out_specs=out_specs,
        out_shape=out_shape,
        compiler_params=pltpu.CompilerParams(dimension_semantics=("arbitrary", "arbitrary"),
                                             vmem_limit_bytes=48 * 1024 * 1024),
        name="out_proj",
    )(*ys, sg, x, cx, modr, wo, lng, lnb)
    return (res[0], res[1]) if ctx_out else (res[0], None)


def _in_perm():
    hp = np.array(HEAD_PERM)
    head_cols = (hp[:, None] * HEAD_DIM + np.arange(HEAD_DIM)[None, :]).reshape(-1)
    idx = np.arange(4 * GROUP_W + 2 * GROUP_W + 3 * GROUP_W + 2 * GROUP_W + D_MODEL)
    b0 = A_W
    d0 = A_W + 2 * GROUP_W + 3 * GROUP_W
    idx[b0:b0 + GROUP_W] = b0 + head_cols
    idx[d0:d0 + GROUP_W] = d0 + head_cols
    idx[GATE_OFF + GROUP_W:GATE_OFF + 2 * GROUP_W] = GATE_OFF + GROUP_W + head_cols
    idx[GATE_OFF + 3 * GROUP_W:GATE_OFF + 4 * GROUP_W] = GATE_OFF + 3 * GROUP_W + head_cols
    return idx


def _out_perm():
    hp = np.array(HEAD_PERM)
    head_rows = (hp[:, None] * HEAD_DIM + np.arange(HEAD_DIM)[None, :]).reshape(-1)
    idx = np.arange(D_MODEL)
    idx[GROUP_W:2 * GROUP_W] = GROUP_W + head_rows
    idx[3 * GROUP_W:4 * GROUP_W] = 3 * GROUP_W + head_rows
    return idx


def _rope_tables(n_lat, n_ctx, dim):
    rows = n_lat // GRID_W
    row = jnp.repeat(jnp.arange(rows, dtype=F32), GRID_W)
    col = jnp.broadcast_to(jnp.arange(GRID_W, dtype=F32), (rows, GRID_W)).reshape(-1)
    d_axis = dim // 2
    inv = ROPE_THETA ** (-jnp.arange(0, d_axis, 2, dtype=F32) / d_axis)
    ang_r = row[:, None] * inv
    ang_c = col[:, None] * inv
    ang = jnp.concatenate([ang_r, ang_r, ang_c, ang_c], axis=-1)
    cos, sin = jnp.cos(ang), jnp.sin(ang)
    first = (np.arange(dim) % (dim // 2)) < (dim // 4)
    sa = jnp.where(first[None, :], -sin, 0.0)
    sb = jnp.where(first[None, :], 0.0, sin)
    reps = LANES // dim
    pad = lambda a, fill: jnp.concatenate([jnp.full((n_ctx, LANES), fill, F32), jnp.tile(a, (1, reps))], axis=0)
    return pad(cos, 1.0), pad(sa, 0.0), pad(sb, 0.0)


def kernel(x, c, ctx, c_ctx, w_in, w_out, w_ada, b_ada, ln_g, ln_b, hg_lb_logits, hg_norm_g, ga_q_norm_g,
           ga_k_norm_g, df_lambda, df_subln_g, wn_sink):
    bsz, s, d = x.shape
    lc = ctx.shape[1]
    depth = w_in.shape[0]
    assert d == D_MODEL and s % TM == 0 and lc % TM == 0 and s >= TM + 2 * WINDOW and bsz <= MOD_ROWS - 8
    assert w_in.shape[2] == IN_W

    cc = jnp.concatenate([c, jnp.zeros((MOD_ROWS - 8 - bsz, d), F32), c_ctx[None, :],
                          jnp.zeros((7, d), F32)], axis=0)
    mod = _ada_call(cc, w_ada, b_ada)
    modr = mod.reshape(depth * MOD_ROWS, 1, 3 * d)

    tabs = _rope_tables(s, lc, HEAD_DIM) + _rope_tables(s, lc, DF_QK)
    wp = w_in[:, :, _in_perm()].astype(BF16)
    wo = w_out[:, _out_perm(), :].astype(BF16)
    pmf = jnp.asarray(_hgrn_sum_matrix(False), BF16)
    pmb = jnp.asarray(_hgrn_sum_matrix(True), BF16)
    tile2 = lambda v: jnp.tile(v, 2)[None, :]
    alpha = (2.0 * depth) ** 0.25

    cx = ctx
    for l in range(depth):
        ctx_out = l < depth - 1
        za, at, sg = _inproj_call(x, cx, modr, l, wp[l], tile2(ga_q_norm_g[l]), tile2(ga_k_norm_g[l]), tabs)
        ya = _hgrn_call(za, hg_lb_logits, jnp.tile(hg_norm_g[l], 4)[None, :], pmf, pmb, l, lc)
        lam_init = 0.8 - 0.6 * math.exp(-0.3 * l)
        yb = _attn_call("B", at, lc, ctx_out)
        yc = _attn_call("C", at, lc, ctx_out, extra=(df_lambda[l], tile2(df_subln_g[l])), lam_init=lam_init)
        yd = _attn_call("D", at, lc, ctx_out, extra=(wn_sink[l],))
        x, cx_new = _outproj_call((ya, yb, yc, yd), sg, x, cx, modr, l, wo[l],
                                  ln_g[l][None, :], ln_b[l][None, :], ctx_out, alpha)
        cx = cx_new if ctx_out else cx
    return x
```

```python
import functools
import math

import numpy as np
import jax
import jax.numpy as jnp
from jax import lax
from jax.experimental import pallas as pl
from jax.experimental.pallas import tpu as pltpu

F32 = jnp.float32
BF16 = jnp.bfloat16

D_MODEL = 1024
GROUP_W = 256
HEAD_DIM = 64
DF_QK = 32
GRID_W = 64
WINDOW = 128
ROPE_THETA = 10000.0
LN_EPS = 1e-5
RMS_EPS = 1e-6
LOG2E = 1.0 / math.log(2.0)
HG_CHUNK = 64
HG_SUB = 8
HG_UNROLL = 4

LANES = 128
TM = 256
IN_SUB = 3
MOD_ROWS = 24

A_W = 4 * GROUP_W
ATT_W = 14 * LANES
ATT_OFF = A_W
GATE_OFF = A_W + ATT_W
IN_W = GATE_OFF + D_MODEL
B_Q, B_K, B_V = 0, 2, 3
C_Q, C_K, C_V = 4, 6, 8
D_Q, D_K, D_V = 10, 12, 13
HEAD_PERM = (0, 2, 1, 3)

NT = (((1,), (1,)), ((), ()))
TN = (((0,), (0,)), ((), ()))


def _dot(a, b, dims=None):
    if dims is None:
        return jnp.dot(a, b, preferred_element_type=F32)
    return lax.dot_general(a, b, dims, preferred_element_type=F32)


def _split2(x):
    hi = x.astype(BF16)
    lo = (x - hi.astype(F32)).astype(BF16)
    return hi, lo


def _seg_sum(x, ones_bd):
    hi, lo = _split2(x)
    return _dot(hi, ones_bd) + _dot(lo, ones_bd)


def _block_ones(n, seg):
    r = lax.broadcasted_iota(jnp.int32, (n, n), 0) // seg
    c = lax.broadcasted_iota(jnp.int32, (n, n), 1) // seg
    return jnp.where(r == c, 1.0, 0.0).astype(BF16)


def _silu(x):
    return x * jax.nn.sigmoid(x)


def _ada_kernel(c_ref, w_ref, b_ref, o_ref):
    s = _silu(c_ref[...])
    o_ref[0] = jnp.dot(s, w_ref[0], preferred_element_type=F32, precision=lax.Precision.HIGHEST) + b_ref[0]


def _ada_call(cc, w_ada, b_ada):
    depth = w_ada.shape[0]
    n3 = w_ada.shape[2]
    nb = n3 // D_MODEL
    return pl.pallas_call(
        _ada_kernel,
        grid=(depth, nb),
        in_specs=[
            pl.BlockSpec((MOD_ROWS, D_MODEL), lambda l, j: (0, 0)),
            pl.BlockSpec((1, D_MODEL, D_MODEL), lambda l, j: (l, 0, j)),
            pl.BlockSpec((1, 1, D_MODEL), lambda l, j: (l, 0, j)),
        ],
        out_specs=pl.BlockSpec((1, MOD_ROWS, D_MODEL), lambda l, j: (l, 0, j)),
        out_shape=jax.ShapeDtypeStruct((depth, MOD_ROWS, n3), F32),
        compiler_params=pltpu.CompilerParams(dimension_semantics=("arbitrary", "arbitrary"),
                                             vmem_limit_bytes=32 * 1024 * 1024),
        name="ada_mod",
    )(cc, w_ada, b_ada.reshape(depth, 1, n3))


def _rope(x, cos, sa, sb, shift):
    up = pltpu.roll(x, LANES - shift, axis=1)
    dn = pltpu.roll(x, shift, axis=1)
    return x * cos + up * sa + dn * sb


def _inproj_kernel(xa_ref, xb_ref, xc_ref, c_ref, shc_ref, scc_ref, sh_ref, sc_ref, w_ref, qn_ref, kn_ref,
                   c64_ref, sa64_ref, sb64_ref, c32_ref, sa32_ref, sb32_ref,
                   za_ref, at_ref, sg_ref, h_scr):
    t = pl.program_id(1)
    sc = 1.0 + sc_ref[0]
    sh = sh_ref[0]

    @pl.when(t == 0)
    def _():
        h_scr[0:TM, :] = (c_ref[0] * (1.0 + scc_ref[0]) + shc_ref[0]).astype(BF16)

    @pl.when(t > 0)
    def _():
        h_scr[0:TM, :] = (xa_ref[0] * sc + sh).astype(BF16)

    h_scr[TM:2 * TM, :] = (xb_ref[0] * sc + sh).astype(BF16)
    h_scr[2 * TM:3 * TM, :] = (xc_ref[0] * sc + sh).astype(BF16)
    h = h_scr[...]

    def proj(blk0, nblk):
        c0 = ATT_OFF + blk0 * LANES
        return _dot(h, w_ref[:, c0:c0 + nblk * LANES])

    def put(blk, val):
        at_ref[0, :, blk * LANES:(blk + 1) * LANES] = val.astype(BF16)

    def lanes(a, j):
        return a[:, j * LANES:(j + 1) * LANES]

    acc = _dot(h, w_ref[:, 0:A_W])
    for j in range(A_W // LANES):
        za_ref[0, j] = lanes(acc, j)

    ones64 = _block_ones(LANES, HEAD_DIM)
    c64, sa64, sb64 = c64_ref[...], sa64_ref[...], sb64_ref[...]
    c32, sa32, sb32 = c32_ref[...], sa32_ref[...], sb32_ref[...]
    q64 = HEAD_DIM // 4
    q32 = DF_QK // 4

    def rms(v, g):
        ms = _seg_sum(v * v, ones64) * (1.0 / HEAD_DIM)
        return v * lax.rsqrt(ms + RMS_EPS) * g

    acc = proj(B_Q, 4)
    for j in range(2):
        put(B_Q + j, _rope(rms(lanes(acc, j), qn_ref[...]), c64, sa64, sb64, q64) * (HEAD_DIM ** -0.5 * LOG2E))
    put(B_K, _rope(rms(lanes(acc, 2), kn_ref[...]), c64, sa64, sb64, q64))
    put(B_V, lanes(acc, 3))

    acc = proj(C_Q, 6)
    for j in range(2):
        put(C_Q + j, _rope(lanes(acc, j), c32, sa32, sb32, q32) * (DF_QK ** -0.5 * LOG2E))
        put(C_K + j, _rope(lanes(acc, 2 + j), c32, sa32, sb32, q32))
        put(C_V + j, lanes(acc, 4 + j))

    acc = proj(D_Q, 4)
    for j in range(2):
        put(D_Q + j, _rope(lanes(acc, j), c64, sa64, sb64, q64) * (HEAD_DIM ** -0.5 * LOG2E))
    put(D_K, _rope(lanes(acc, 2), c64, sa64, sb64, q64))
    put(D_V, lanes(acc, 3))

    sg_ref[0] = _silu(_dot(h, w_ref[:, GATE_OFF:IN_W])).astype(sg_ref.dtype)


def _inproj_call(x, cx, modr, layer, wp, qn, kn, tabs):
    bsz, s, _ = x.shape
    lc = cx.shape[1]
    nlt = s // TM
    t_tot = lc + s
    rows = IN_SUB * TM
    assert lc == TM and t_tot % rows == 0
    tab_spec = pl.BlockSpec((rows, LANES), lambda b, t: (t, 0))
    vec_spec = pl.BlockSpec((1, LANES), lambda b, t: (0, 0))

    def x_spec(off):
        return pl.BlockSpec((1, TM, D_MODEL), lambda b, t: (b, jnp.clip(IN_SUB * t - 1 + off, 0, nlt - 1), 0))

    def mod_spec(col, ctx_row):
        return pl.BlockSpec((1, 1, D_MODEL),
                            lambda b, t: (layer * MOD_ROWS + (MOD_ROWS - 8 if ctx_row else b), 0, col))

    return pl.pallas_call(
        _inproj_kernel,
        grid=(bsz, t_tot // rows),
        in_specs=[
            x_spec(0), x_spec(1), x_spec(2),
            pl.BlockSpec((1, TM, D_MODEL), lambda b, t: (b, 0, 0)),
            mod_spec(0, True), mod_spec(1, True), mod_spec(0, False), mod_spec(1, False),
            pl.BlockSpec((D_MODEL, IN_W), lambda b, t: (0, 0), pipeline_mode=pl.Buffered(1)),
            vec_spec, vec_spec,
            tab_spec, tab_spec, tab_spec, tab_spec, tab_spec, tab_spec,
        ],
        out_specs=[
            pl.BlockSpec((1, A_W // LANES, rows, LANES), lambda b, t: (b, 0, t, 0)),
            pl.BlockSpec((1, rows, ATT_W), lambda b, t: (b, t, 0)),
            pl.BlockSpec((1, rows, D_MODEL), lambda b, t: (b, t, 0)),
        ],
        out_shape=[
            jax.ShapeDtypeStruct((bsz, A_W // LANES, t_tot, LANES), F32),
            jax.ShapeDtypeStruct((bsz, t_tot, ATT_W), BF16),
            jax.ShapeDtypeStruct((bsz, t_tot, D_MODEL), BF16),
        ],
        scratch_shapes=[pltpu.VMEM((rows, D_MODEL), BF16)],
        compiler_params=pltpu.CompilerParams(dimension_semantics=("arbitrary", "arbitrary"),
                                             vmem_limit_bytes=56 * 1024 * 1024),
        name="in_proj",
    )(x, x, x, cx, modr, modr, modr, modr, wp, qn, kn, *tabs)


def _hgrn_levels():
    w = HG_CHUNK // 2
    out = []
    while w >= HG_SUB:
        out.append(w)
        w //= 2
    return tuple(out)


def _hgrn_time_of_row():
    nsub = HG_CHUNK // HG_SUB
    r = np.arange(HG_CHUNK)
    return (r % nsub) * HG_SUB + r // nsub


def _hgrn_sum_matrix(rev):
    c = HG_CHUNK
    t = np.arange(c)[:, None]
    j = np.arange(c)[None, :]
    mats = [j <= t, j > t]
    for w in _hgrn_levels():
        m = (t // (2 * w)) * 2 * w + w
        mats.append(np.where(t >= m, (j >= m) & (j <= t), (j > t) & (j <= m - 1)))
    if rev:
        mats = [m[::-1, ::-1] for m in mats]
    tor = _hgrn_time_of_row()
    full = np.concatenate([m[tor][:, tor] for m in mats], axis=0).astype(np.float32)
    return np.concatenate([full, full, full], axis=1)


def _hgrn_kernel(layer, depth, nc_ctx, za_ref, lb_ref, g_ref, pmf_ref, pmb_ref, y_ref,
                 of_scr, ob_scr, stf_scr, stb_scr):
    c = HG_CHUNK
    u = HG_SUB
    nsub = c // u
    w4 = GROUP_W
    levels = _hgrn_levels()
    t_tot = za_ref.shape[2]
    nc_tot = t_tot // c
    log2e = 1.0 / math.log(2.0)

    lbl = [lb_ref[j] for j in range(depth)]
    mx = functools.reduce(jnp.maximum, lbl)
    ex = [jnp.exp(v - mx) for v in lbl]
    den = functools.reduce(lambda a, b: a + b, ex)
    lower = jnp.zeros_like(den)
    for j in range(1, layer + 1):
        lower = lower + ex[j] / den

    r256 = lax.broadcasted_iota(jnp.int32, (w4, w4), 0)
    c256 = lax.broadcasted_iota(jnp.int32, (w4, w4), 1)
    headmask = (r256 // HEAD_DIM) == (c256 // HEAD_DIM)
    ones_bd = jnp.where(headmask, 1.0, 0.0).astype(BF16)
    rrow = lax.broadcasted_iota(jnp.int32, (c, w4), 0)
    trow = (rrow % nsub) * u + rrow // nsub
    rcol = lax.broadcasted_iota(jnp.int32, (c, w4), 1) % HEAD_DIM
    scol = (rcol % nsub) * u + rcol // nsub

    def load(r0, slab0):
        return jnp.concatenate(
            [jnp.concatenate([za_ref[0, slab0 + j, pl.ds(r0 + p, nsub, stride=u), :] for p in range(u)], axis=0)
             for j in range(2)], axis=1)

    def chunk(r0, rev, st_scr):
        zq = load(r0, 0)
        v = load(r0, 2)
        zf = load(r0, 6 if rev else 4)
        lb = lower[1:2] if rev else lower[0:1]
        q = _silu(zq)
        f = lb + (1.0 - lb) * jax.nn.sigmoid(zf)
        k = 1.0 - f
        lf2 = jnp.log(f) * log2e

        hi = lf2.astype(BF16)
        r1 = lf2 - hi.astype(F32)
        mid = r1.astype(BF16)
        lo = (r1 - mid.astype(F32)).astype(BF16)
        pm = pmb_ref[...] if rev else pmf_ref[...]
        esum = _dot(pm, jnp.concatenate([hi, mid, lo], axis=0))
        e_cum = esum[0:c]
        e_after = esum[c:2 * c]
        e_tot = e_cum[0:1] if rev else e_cum[c - 1:c]

        st = st_scr[...]
        vb = v.astype(BF16)
        o = _dot((q * jnp.exp2(e_cum)).astype(BF16), st.astype(BF16), NT)

        att = jnp.zeros((c, w4), F32)
        for li, w in enumerate(levels):
            ew = jnp.exp2(esum[(2 + li) * c:(3 + li) * c])
            is_q = ((trow // w) % 2 == 0) if rev else ((trow // w) % 2 == 1)
            qw = jnp.where(is_q, q * ew, 0.0).astype(BF16)
            kw = jnp.where(is_q, 0.0, k * ew).astype(BF16)
            kbd = jnp.where(headmask, jnp.concatenate([kw] * 4, axis=0), jnp.zeros((), BF16))
            a = _dot(qw, kbd, NT)
            att = att + jnp.where((trow // (2 * w)) == (scol // (2 * w)), a, 0.0)
        vbd = jnp.where(headmask, jnp.concatenate([vb] * 4, axis=0), jnp.zeros((), BF16))
        o = o + _dot(att.astype(BF16), vbd)

        grp = lambda a, p: a[p * nsub:(p + 1) * nsub]
        pairs = [(p, s) for p in range(u) for s in range(u) if (s >= p if rev else s <= p)]
        tiles = []
        for p, s in pairs:
            qk = grp(q, p) * grp(k, s)
            tiles.append(qk if p == s else qk * jnp.exp2(grp(e_cum, p) - grp(e_cum, s)))
        red = _dot(jnp.concatenate(tiles, axis=0).astype(BF16), ones_bd)
        ods = [None] * u
        for i, (p, s) in enumerate(pairs):
            term = red[i * nsub:(i + 1) * nsub] * grp(v, s)
            ods[p] = term if ods[p] is None else ods[p] + term
        o = o + jnp.concatenate(ods, axis=0)

        upd = _dot(vb, (k * jnp.exp2(e_after)).astype(BF16), TN)
        st_scr[...] = st * jnp.exp2(e_tot) + jnp.where(headmask, upd, 0.0)
        return o

    def store(o_scr, r0, o):
        for j in range(2):
            for p in range(u):
                o_scr[j, pl.ds(r0 + p, nsub, stride=u), :] = o[p * nsub:(p + 1) * nsub, j * LANES:(j + 1) * LANES]

    stf_scr[...] = jnp.zeros_like(stf_scr)
    stb_scr[...] = jnp.zeros_like(stb_scr)

    def body(i2, carry):
        for jj in range(HG_UNROLL):
            i = i2 * HG_UNROLL + jj
            r0 = pl.multiple_of(i * c, c)
            store(of_scr, r0, chunk(r0, False, stf_scr))
            ci = jnp.where(i < nc_ctx, nc_ctx - 1 - i, nc_tot - 1 - (i - nc_ctx))
            r1 = pl.multiple_of(ci * c, c)
            store(ob_scr, r1, chunk(r1, True, stb_scr))
        return carry

    lax.fori_loop(0, nc_tot // HG_UNROLL, body, 0)

    g = g_ref[...]
    nrow = 4 * c

    def norm_body(i, carry):
        r0 = pl.multiple_of(i * nrow, nrow)
        o = jnp.concatenate([of_scr[j, pl.ds(r0, nrow), :] + ob_scr[j, pl.ds(r0, nrow), :] for j in range(2)], axis=1)
        ms = _dot((o * o).astype(BF16), ones_bd) * (1.0 / HEAD_DIM)
        y_ref[0, pl.ds(r0, nrow), :] = (o * lax.rsqrt(ms + RMS_EPS) * g).astype(y_ref.dtype)
        return carry

    lax.fori_loop(0, t_tot // nrow, norm_body, 0)


def _hgrn_call(za, lb_logits, g4, pmf, pmb, layer, lc):
    bsz, nslab, t_tot, _ = za.shape
    depth = lb_logits.shape[0]
    return pl.pallas_call(
        functools.partial(_hgrn_kernel, layer, depth, lc // HG_CHUNK),
        grid=(bsz,),
        in_specs=[
            pl.BlockSpec((1, nslab, t_tot, LANES), lambda b: (b, 0, 0, 0)),
            pl.BlockSpec((depth, 2, GROUP_W), lambda b: (0, 0, 0)),
            pl.BlockSpec((1, GROUP_W), lambda b: (0, 0)),
            pl.BlockSpec(pmf.shape, lambda b: (0, 0)),
            pl.BlockSpec(pmb.shape, lambda b: (0, 0)),
        ],
        out_specs=pl.BlockSpec((1, t_tot, GROUP_W), lambda b: (b, 0, 0)),
        out_shape=jax.ShapeDtypeStruct((bsz, t_tot, GROUP_W), BF16),
        scratch_shapes=[pltpu.VMEM((2, t_tot, LANES), F32), pltpu.VMEM((2, t_tot, LANES), F32),
                        pltpu.VMEM((GROUP_W, GROUP_W), F32), pltpu.VMEM((GROUP_W, GROUP_W), F32)],
        compiler_params=pltpu.CompilerParams(dimension_semantics=("arbitrary",),
                                             vmem_limit_bytes=48 * 1024 * 1024),
        name="hgrn2",
    )(za, lb_logits, g4, pmf, pmb)


def _lane_iota(rows):
    return lax.broadcasted_iota(jnp.int32, (rows, LANES), 1)


def _mask_q(q, lane, off, width):
    return jnp.where((lane >= off) & (lane < off + width), q, jnp.zeros((), q.dtype))


def _fill_v_ones(vx_scr, v_ref):
    @pl.when(pl.program_id(1) == 0)
    def _():
        vx_scr[:, 0:LANES] = v_ref[0]
        vx_scr[:, LANES:2 * LANES] = jnp.ones((v_ref.shape[1], LANES), BF16)


def _attn_b_kernel(tile0, nct, q_ref, k_ref, v_ref, o_ref, vx_scr, l_scr, e_scr):
    t = pl.program_id(1) + tile0
    lane = _lane_iota(TM)
    t_tot = k_ref.shape[1]
    _fill_v_ones(vx_scr, v_ref)

    def run(nk):
        def logits(hd):
            j, half = divmod(hd, 2)
            q = q_ref[0, :, j * LANES:(j + 1) * LANES]
            l_scr[hd, :, 0:nk] = _dot(_mask_q(q, lane, half * HEAD_DIM, HEAD_DIM), k_ref[0, 0:nk, :], NT)

        def softmax(hd):
            mx = jnp.max(l_scr[hd, :, 0:nk], axis=-1, keepdims=True)
            e_scr[hd, :, 0:nk] = jnp.exp2(l_scr[hd, :, 0:nk] - mx).astype(BF16)

        def values(hd):
            acc = _dot(e_scr[hd, :, 0:nk], vx_scr[0:nk, :])
            return acc[:, 0:LANES] / acc[:, LANES:2 * LANES]

        nh = 4
        outs = [None] * nh
        logits(0)
        for hd in range(nh):
            if hd + 1 < nh:
                logits(hd + 1)
            softmax(hd)
            outs[hd] = values(hd)
        for j in range(2):
            o_ref[0, :, j * LANES:(j + 1) * LANES] = jnp.where(
                lane < HEAD_DIM, outs[2 * j], outs[2 * j + 1]).astype(o_ref.dtype)

    if tile0 < nct:
        @pl.when(t < nct)
        def _():
            run(nct * TM)

    @pl.when(t >= nct)
    def _():
        run(t_tot)


def _attn_c_kernel(tile0, nct, lam_init, q_ref, k_ref, v_ref, lam_ref, g_ref, o_ref, l_scr, e_scr):
    t = pl.program_id(1) + tile0
    lane = _lane_iota(TM)
    t_tot = k_ref.shape[1]
    lp = lam_ref[...]
    lam = (jnp.exp(jnp.sum(lp[0:1] * lp[1:2], axis=-1, keepdims=True))
           - jnp.exp(jnp.sum(lp[2:3] * lp[3:4], axis=-1, keepdims=True)) + lam_init)
    ones64 = _block_ones(LANES, HEAD_DIM)

    def run(nk):
        def logits(hd):
            for m in range(2):
                grp = 2 * hd + m
                blk, off = grp // 4, (grp % 4) * DF_QK
                q = q_ref[0, :, blk * LANES:(blk + 1) * LANES]
                l_scr[grp, :, 0:nk] = _dot(_mask_q(q, lane, off, DF_QK),
                                           k_ref[0, 0:nk, blk * LANES:(blk + 1) * LANES], NT)

        def softmax(hd):
            sums = []
            for m in range(2):
                grp = 2 * hd + m
                mx = jnp.max(l_scr[grp, :, 0:nk], axis=-1, keepdims=True)
                l_scr[grp, :, 0:nk] = jnp.exp2(l_scr[grp, :, 0:nk] - mx)
                sums.append(jnp.sum(l_scr[grp, :, 0:nk], axis=-1, keepdims=True))
            wgt = l_scr[2 * hd, :, 0:nk] * (1.0 / sums[0]) - l_scr[2 * hd + 1, :, 0:nk] * (lam / sums[1])
            e_scr[hd, :, 0:nk] = wgt.astype(BF16)

        def values(hd):
            hp = hd // 2
            return _dot(e_scr[hd, :, 0:nk], v_ref[0, 0:nk, hp * LANES:(hp + 1) * LANES])

        nh = 4
        outs = [None] * nh
        logits(0)
        for hd in range(nh):
            if hd + 1 < nh:
                logits(hd + 1)
            softmax(hd)
            outs[hd] = values(hd)
        for hp in range(2):
            o = jnp.where(lane < HEAD_DIM, outs[2 * hp], outs[2 * hp + 1])
            ms = _seg_sum(o * o, ones64) * (1.0 / HEAD_DIM)
            y = o * lax.rsqrt(ms + RMS_EPS) * g_ref[...] * (1.0 - lam_init)
            o_ref[0, :, hp * LANES:(hp + 1) * LANES] = y.astype(o_ref.dtype)

    if tile0 < nct:
        @pl.when(t < nct)
        def _():
            run(nct * TM)

    @pl.when(t >= nct)
    def _():
        run(t_tot)


def _attn_d_kernel(tile0, nct, sink_ref, q_ref, k_ref, v_ref, o_ref, vx_scr, l_scr, e_scr):
    t = pl.program_id(1) + tile0
    lane = _lane_iota(TM)
    lc = nct * TM
    s_lat = k_ref.shape[1] - lc
    win = TM + 2 * WINDOW
    _fill_v_ones(vx_scr, v_ref)

    def run(latent):
        ncol = lc + win if latent else lc
        if latent:
            base = (t - nct) * TM
            wstart = pl.multiple_of(jnp.clip(base - WINDOW, 0, s_lat - win), WINDOW)
            qpos = base + lax.broadcasted_iota(jnp.int32, (TM, win), 0)
            kpos = wstart + lax.broadcasted_iota(jnp.int32, (TM, win), 1)
            allowed = jnp.abs(kpos - qpos) <= WINDOW

        def logits(hd):
            j, half = divmod(hd, 2)
            qm = _mask_q(q_ref[0, :, j * LANES:(j + 1) * LANES], lane, half * HEAD_DIM, HEAD_DIM)
            l_scr[hd, :, 0:lc] = _dot(qm, k_ref[0, 0:lc, :], NT)
            if latent:
                lw = _dot(qm, k_ref[0, pl.ds(lc + wstart, win), :], NT)
                l_scr[hd, :, lc:ncol] = jnp.where(allowed, lw, -jnp.inf)

        def softmax(hd):
            sink = sink_ref[HEAD_PERM[hd]] * LOG2E
            mx = jnp.maximum(jnp.max(l_scr[hd, :, 0:ncol], axis=-1, keepdims=True), sink)
            e_scr[hd, :, 0:ncol] = jnp.exp2(l_scr[hd, :, 0:ncol] - mx).astype(BF16)
            return jnp.exp2(sink - mx)

        def values(hd, e_sink):
            acc = _dot(e_scr[hd, :, 0:lc], vx_scr[0:lc, :])
            if latent:
                acc = acc + _dot(e_scr[hd, :, lc:ncol], vx_scr[pl.ds(lc + wstart, win), :])
            return acc[:, 0:LANES] / (acc[:, LANES:2 * LANES] + e_sink)

        nh = 4
        outs = [None] * nh
        logits(0)
        for hd in range(nh):
            if hd + 1 < nh:
                logits(hd + 1)
            outs[hd] = values(hd, softmax(hd))
        for j in range(2):
            o_ref[0, :, j * LANES:(j + 1) * LANES] = jnp.where(
                lane < HEAD_DIM, outs[2 * j], outs[2 * j + 1]).astype(o_ref.dtype)

    if tile0 < nct:
        @pl.when(t < nct)
        def _():
            run(False)

    @pl.when(t >= nct)
    def _():
        run(True)


def _attn_call(kind, at, lc, ctx_out, extra=(), lam_init=None):
    bsz, t_tot, _ = at.shape
    nct = lc // TM
    ntile = t_tot // TM
    tile0 = 0 if ctx_out else nct
    qb, kb, vb, nkv = {"B": (B_Q, B_K, B_V, 1), "C": (C_Q, C_K, C_V, 2), "D": (D_Q, D_K, D_V, 1)}[kind]
    q_spec = pl.BlockSpec((1, TM, 2 * LANES), lambda b, t: (b, t + tile0, qb // 2))
    k_spec = pl.BlockSpec((1, t_tot, nkv * LANES), lambda b, t: (b, 0, kb // nkv))
    v_spec = pl.BlockSpec((1, t_tot, nkv * LANES), lambda b, t: (b, 0, vb // nkv))
    o_spec = pl.BlockSpec((1, TM, GROUP_W), lambda b, t: (b, t, 0))
    scratch = []
    if kind == "B":
        body = functools.partial(_attn_b_kernel, tile0, nct)
        in_specs, args = [q_spec, k_spec, v_spec], (at, at, at)
        scratch = [pltpu.VMEM((t_tot, 2 * LANES), BF16), pltpu.VMEM((4, TM, t_tot), F32),
                   pltpu.VMEM((4, TM, t_tot), BF16)]
    elif kind == "C":
        body = functools.partial(_attn_c_kernel, tile0, nct, lam_init)
        lam, g2 = extra
        in_specs = [q_spec, k_spec, v_spec,
                    pl.BlockSpec(lam.shape, lambda b, t: (0, 0)), pl.BlockSpec((1, LANES), lambda b, t: (0, 0))]
        args = (at, at, at, lam, g2)
        scratch = [pltpu.VMEM((8, TM, t_tot), F32), pltpu.VMEM((4, TM, t_tot), BF16)]
    else:
        body = functools.partial(_attn_d_kernel, tile0, nct)
        (sink,) = extra
        in_specs = [pl.BlockSpec(memory_space=pltpu.SMEM), q_spec, k_spec, v_spec]
        args = (sink, at, at, at)
        ncol = lc + TM + 2 * WINDOW
        scratch = [pltpu.VMEM((t_tot, 2 * LANES), BF16), pltpu.VMEM((4, TM, ncol), F32),
                   pltpu.VMEM((4, TM, ncol), BF16)]
    return pl.pallas_call(
        body,
        grid=(bsz, ntile - tile0),
        in_specs=in_specs,
        out_specs=o_spec,
        out_shape=jax.ShapeDtypeStruct((bsz, t_tot - tile0 * TM, GROUP_W), BF16),
        scratch_shapes=scratch,
        compiler_params=pltpu.CompilerParams(dimension_semantics=("arbitrary", "arbitrary"),
                                             vmem_limit_bytes=48 * 1024 * 1024),
        name="attn_" + kind.lower(),
    )(*args)


def _outproj_kernel(nct, ctx_out, alpha, ya_ref, yb_ref, yc_ref, yd_ref, sg_ref, x_ref, c_ref, gt_ref,
                    w_ref, lg_ref, lb_ref, *out_refs):
    t = pl.program_id(1) + (0 if ctx_out else nct)
    y = jnp.concatenate([ya_ref[0], yb_ref[0], yc_ref[0], yd_ref[0]], axis=-1) * sg_ref[0]
    p = _dot(y, w_ref[...])
    gate = gt_ref[0]

    def fin(res):
        z = alpha * res + gate * p
        mu = jnp.mean(z, axis=-1, keepdims=True)
        zc = z - mu
        var = jnp.mean(zc * zc, axis=-1, keepdims=True)
        return zc * lax.rsqrt(var + LN_EPS) * lg_ref[...] + lb_ref[...]

    if ctx_out:
        @pl.when(t < nct)
        def _():
            out_refs[1][0] = fin(c_ref[0])

    @pl.when(t >= nct)
    def _():
        out_refs[0][0] = fin(x_ref[0])


def _outproj_call(ys, sg, x, cx, modr, layer, wo, lng, lnb, ctx_out, alpha):
    bsz, s, _ = x.shape
    lc = cx.shape[1]
    nct, nlt = lc // TM, s // TM
    tile0 = 0 if ctx_out else nct

    def y_spec(arr):
        off = tile0 if arr.shape[1] == lc + s else 0
        return pl.BlockSpec((1, TM, GROUP_W), lambda b, t: (b, t + off, 0))

    vec = pl.BlockSpec((1, D_MODEL), lambda b, t: (0, 0))
    in_specs = [y_spec(a) for a in ys] + [
        pl.BlockSpec((1, TM, D_MODEL), lambda b, t: (b, t + tile0, 0)),
        pl.BlockSpec((1, TM, D_MODEL), lambda b, t: (b, jnp.maximum(t + tile0 - nct, 0), 0)),
        pl.BlockSpec((1, TM, D_MODEL), lambda b, t: (b, jnp.minimum(t + tile0, nct - 1), 0)),
        pl.BlockSpec((1, 1, D_MODEL),
                     lambda b, t: (layer * MOD_ROWS + jnp.where(t + tile0 < nct, MOD_ROWS - 8, b), 0, 2)),
        pl.BlockSpec((D_MODEL, D_MODEL), lambda b, t: (0, 0)),
        vec, vec,
    ]
    out_specs = [pl.BlockSpec((1, TM, D_MODEL), lambda b, t: (b, jnp.maximum(t + tile0 - nct, 0), 0))]
    out_shape = [jax.ShapeDtypeStruct((bsz, s, D_MODEL), F32)]
    if ctx_out:
        out_specs.append(pl.BlockSpec((1, TM, D_MODEL), lambda b, t: (b, jnp.minimum(t, nct - 1), 0)))
        out_shape.append(jax.ShapeDtypeStruct((bsz, lc, D_MODEL), F32))
    res = pl.pallas_call(
        functools.partial(_outproj_kernel, nct, ctx_out, alpha),
        grid=(bsz, nct + nlt - tile0),
        in_specs=in_specs,
        out_specs=out_specs,
        out_shape=out_shape,
        compiler_params=pltpu.CompilerParams(dimension_semantics=("arbitrary", "arbitrary"),
                                             vmem_limit_bytes=48 * 1024 * 1024),
        name="out_proj",
    )(*ys, sg, x, cx, modr, wo, lng, lnb)
    return (res[0], res[1]) if ctx_out else (res[0], None)


def _in_perm():
    hp = np.array(HEAD_PERM)
    head_cols = (hp[:, None] * HEAD_DIM + np.arange(HEAD_DIM)[None, :]).reshape(-1)
    idx = np.arange(4 * GROUP_W + 2 * GROUP_W + 3 * GROUP_W + 2 * GROUP_W + D_MODEL)
    b0 = A_W
    d0 = A_W + 2 * GROUP_W + 3 * GROUP_W
    idx[b0:b0 + GROUP_W] = b0 + head_cols
    idx[d0:d0 + GROUP_W] = d0 + head_cols
    idx[GATE_OFF + GROUP_W:GATE_OFF + 2 * GROUP_W] = GATE_OFF + GROUP_W + head_cols
    idx[GATE_OFF + 3 * GROUP_W:GATE_OFF + 4 * GROUP_W] = GATE_OFF + 3 * GROUP_W + head_cols
    return idx


def _out_perm():
    hp = np.array(HEAD_PERM)
    head_rows = (hp[:, None] * HEAD_DIM + np.arange(HEAD_DIM)[None, :]).reshape(-1)
    idx = np.arange(D_MODEL)
    idx[GROUP_W:2 * GROUP_W] = GROUP_W + head_rows
    idx[3 * GROUP_W:4 * GROUP_W] = 3 * GROUP_W + head_rows
    return idx


def _take_runs(a, idx, axis):
    idx = [int(v) for v in idx]
    runs, start = [], idx[0]
    for prev, cur in zip(idx, idx[1:]):
        if cur != prev + 1:
            runs.append((start, prev + 1))
            start = cur
    runs.append((start, idx[-1] + 1))
    return jnp.concatenate([lax.slice_in_dim(a, lo, hi, axis=axis) for lo, hi in runs], axis=axis)


def _rope_tables(n_lat, n_ctx, dim):
    rows = n_lat // GRID_W
    row = jnp.repeat(jnp.arange(rows, dtype=F32), GRID_W)
    col = jnp.broadcast_to(jnp.arange(GRID_W, dtype=F32), (rows, GRID_W)).reshape(-1)
    d_axis = dim // 2
    inv = ROPE_THETA ** (-jnp.arange(0, d_axis, 2, dtype=F32) / d_axis)
    ang_r = row[:, None] * inv
    ang_c = col[:, None] * inv
    ang = jnp.concatenate([ang_r, ang_r, ang_c, ang_c], axis=-1)
    cos, sin = jnp.cos(ang), jnp.sin(ang)
    first = (np.arange(dim) % (dim // 2)) < (dim // 4)
    sa = jnp.where(first[None, :], -sin, 0.0)
    sb = jnp.where(first[None, :], 0.0, sin)
    reps = LANES // dim
    pad = lambda a, fill: jnp.concatenate([jnp.full((n_ctx, LANES), fill, F32), jnp.tile(a, (1, reps))], axis=0)
    return pad(cos, 1.0), pad(sa, 0.0), pad(sb, 0.0)


def kernel(x, c, ctx, c_ctx, w_in, w_out, w_ada, b_ada, ln_g, ln_b, hg_lb_logits, hg_norm_g, ga_q_norm_g,
           ga_k_norm_g, df_lambda, df_subln_g, wn_sink):
    bsz, s, d = x.shape
    lc = ctx.shape[1]
    depth = w_in.shape[0]
    assert d == D_MODEL and s % TM == 0 and lc % TM == 0 and s >= TM + 2 * WINDOW and bsz <= MOD_ROWS - 8
    assert w_in.shape[2] == IN_W

    cc = jnp.concatenate([c, jnp.zeros((MOD_ROWS - 8 - bsz, d), F32), c_ctx[None, :],
                          jnp.zeros((7, d), F32)], axis=0)
    mod = _ada_call(cc, w_ada, b_ada)
    modr = mod.reshape(depth * MOD_ROWS, 1, 3 * d)

    tabs = _rope_tables(s, lc, HEAD_DIM) + _rope_tables(s, lc, DF_QK)
    wp = _take_runs(w_in.astype(BF16), _in_perm(), 2)
    wo = _take_runs(w_out.astype(BF16), _out_perm(), 1)
    pmf = jnp.asarray(_hgrn_sum_matrix(False), BF16)
    pmb = jnp.asarray(_hgrn_sum_matrix(True), BF16)
    tile2 = lambda v: jnp.tile(v, 2)[None, :]
    alpha = (2.0 * depth) ** 0.25

    cx = ctx
    for l in range(depth):
        ctx_out = l < depth - 1
        za, at, sg = _inproj_call(x, cx, modr, l, wp[l], tile2(ga_q_norm_g[l]), tile2(ga_k_norm_g[l]), tabs)
        ya = _hgrn_call(za, hg_lb_logits, jnp.tile(hg_norm_g[l], 4)[None, :], pmf, pmb, l, lc)
        lam_init = 0.8 - 0.6 * math.exp(-0.3 * l)
        yb = _attn_call("B", at, lc, ctx_out)
        yc = _attn_call("C", at, lc, ctx_out, extra=(df_lambda[l], tile2(df_subln_g[l])), lam_init=lam_init)
        yd = _attn_call("D", at, lc, ctx_out, extra=(wn_sink[l],))
        x, cx_new = _outproj_call((ya, yb, yc, yd), sg, x, cx, modr, l, wo[l],
                                  ln_g[l][None, :], ln_b[l][None, :], ctx_out, alpha)
        cx = cx_new if ctx_out else cx
    return x
```

```python
import functools
import math

import numpy as np
import jax
import jax.numpy as jnp
from jax import lax
from jax.experimental import pallas as pl
from jax.experimental.pallas import tpu as pltpu

F32 = jnp.float32
BF16 = jnp.bfloat16

D_MODEL = 1024
GROUP_W = 256
HEAD_DIM = 64
DF_QK = 32
GRID_W = 64
WINDOW = 128
ROPE_THETA = 10000.0
LN_EPS = 1e-5
RMS_EPS = 1e-6
LOG2E = 1.0 / math.log(2.0)
HG_CHUNK = 64
HG_SUB = 8
HG_UNROLL = 6

LANES = 128
TM = 256
IN_SUB = 3
MOD_ROWS = 24

A_W = 4 * GROUP_W
ATT_W = 14 * LANES
ATT_OFF = A_W
GATE_OFF = A_W + ATT_W
IN_W = GATE_OFF + D_MODEL
B_Q, B_K, B_V = 0, 2, 3
C_Q, C_K, C_V = 4, 6, 8
D_Q, D_K, D_V = 10, 12, 13
HEAD_PERM = (0, 2, 1, 3)

NT = (((1,), (1,)), ((), ()))
TN = (((0,), (0,)), ((), ()))


def _dot(a, b, dims=None):
    if dims is None:
        return jnp.dot(a, b, preferred_element_type=F32)
    return lax.dot_general(a, b, dims, preferred_element_type=F32)


def _split2(x):
    hi = x.astype(BF16)
    lo = (x - hi.astype(F32)).astype(BF16)
    return hi, lo


def _seg_sum(x, ones_bd):
    hi, lo = _split2(x)
    return _dot(hi, ones_bd) + _dot(lo, ones_bd)


def _block_ones(n, seg):
    r = lax.broadcasted_iota(jnp.int32, (n, n), 0) // seg
    c = lax.broadcasted_iota(jnp.int32, (n, n), 1) // seg
    return jnp.where(r == c, 1.0, 0.0).astype(BF16)


def _silu(x):
    return x * jax.nn.sigmoid(x)


def _ada_kernel(c_ref, w_ref, b_ref, o_ref):
    s = _silu(c_ref[...])
    o_ref[0] = jnp.dot(s, w_ref[0], preferred_element_type=F32, precision=lax.Precision.HIGHEST) + b_ref[0]


def _ada_call(cc, w_ada, b_ada):
    depth = w_ada.shape[0]
    n3 = w_ada.shape[2]
    nb = n3 // D_MODEL
    return pl.pallas_call(
        _ada_kernel,
        grid=(depth, nb),
        in_specs=[
            pl.BlockSpec((MOD_ROWS, D_MODEL), lambda l, j: (0, 0)),
            pl.BlockSpec((1, D_MODEL, D_MODEL), lambda l, j: (l, 0, j)),
            pl.BlockSpec((1, 1, D_MODEL), lambda l, j: (l, 0, j)),
        ],
        out_specs=pl.BlockSpec((1, MOD_ROWS, D_MODEL), lambda l, j: (l, 0, j)),
        out_shape=jax.ShapeDtypeStruct((depth, MOD_ROWS, n3), F32),
        compiler_params=pltpu.CompilerParams(dimension_semantics=("arbitrary", "arbitrary"),
                                             vmem_limit_bytes=32 * 1024 * 1024),
        name="ada_mod",
    )(cc, w_ada, b_ada.reshape(depth, 1, n3))


def _rope(x, cos, sa, sb, shift):
    up = pltpu.roll(x, LANES - shift, axis=1)
    dn = pltpu.roll(x, shift, axis=1)
    return x * cos + up * sa + dn * sb


def _inproj_kernel(xa_ref, xb_ref, xc_ref, c_ref, shc_ref, scc_ref, sh_ref, sc_ref, w_ref, qn_ref, kn_ref,
                   c64_ref, sa64_ref, sb64_ref, c32_ref, sa32_ref, sb32_ref,
                   za_ref, at_ref, sg_ref, h_scr):
    t = pl.program_id(1)
    sc = 1.0 + sc_ref[0]
    sh = sh_ref[0]

    @pl.when(t == 0)
    def _():
        h_scr[0:TM, :] = (c_ref[0] * (1.0 + scc_ref[0]) + shc_ref[0]).astype(BF16)

    @pl.when(t > 0)
    def _():
        h_scr[0:TM, :] = (xa_ref[0] * sc + sh).astype(BF16)

    h_scr[TM:2 * TM, :] = (xb_ref[0] * sc + sh).astype(BF16)
    h_scr[2 * TM:3 * TM, :] = (xc_ref[0] * sc + sh).astype(BF16)
    h = h_scr[...]

    def proj(blk0, nblk):
        c0 = ATT_OFF + blk0 * LANES
        return _dot(h, w_ref[:, c0:c0 + nblk * LANES])

    def put(blk, val):
        at_ref[0, :, blk * LANES:(blk + 1) * LANES] = val.astype(BF16)

    def lanes(a, j):
        return a[:, j * LANES:(j + 1) * LANES]

    acc = _dot(h, w_ref[:, 0:A_W])
    for j in range(A_W // LANES):
        za_ref[0, j] = lanes(acc, j)

    ones64 = _block_ones(LANES, HEAD_DIM)
    c64, sa64, sb64 = c64_ref[...], sa64_ref[...], sb64_ref[...]
    c32, sa32, sb32 = c32_ref[...], sa32_ref[...], sb32_ref[...]
    q64 = HEAD_DIM // 4
    q32 = DF_QK // 4

    def rms(v, g):
        ms = _seg_sum(v * v, ones64) * (1.0 / HEAD_DIM)
        return v * lax.rsqrt(ms + RMS_EPS) * g

    acc = proj(B_Q, 4)
    for j in range(2):
        put(B_Q + j, _rope(rms(lanes(acc, j), qn_ref[...]), c64, sa64, sb64, q64) * (HEAD_DIM ** -0.5 * LOG2E))
    put(B_K, _rope(rms(lanes(acc, 2), kn_ref[...]), c64, sa64, sb64, q64))
    put(B_V, lanes(acc, 3))

    acc = proj(C_Q, 6)
    for j in range(2):
        put(C_Q + j, _rope(lanes(acc, j), c32, sa32, sb32, q32) * (DF_QK ** -0.5 * LOG2E))
        put(C_K + j, _rope(lanes(acc, 2 + j), c32, sa32, sb32, q32))
        put(C_V + j, lanes(acc, 4 + j))

    acc = proj(D_Q, 4)
    for j in range(2):
        put(D_Q + j, _rope(lanes(acc, j), c64, sa64, sb64, q64) * (HEAD_DIM ** -0.5 * LOG2E))
    put(D_K, _rope(lanes(acc, 2), c64, sa64, sb64, q64))
    put(D_V, lanes(acc, 3))

    sg_ref[0] = _silu(_dot(h, w_ref[:, GATE_OFF:IN_W])).astype(sg_ref.dtype)


def _inproj_call(x, cx, modr, layer, wp, qn, kn, tabs):
    bsz, s, _ = x.shape
    lc = cx.shape[1]
    nlt = s // TM
    t_tot = lc + s
    rows = IN_SUB * TM
    assert lc == TM and t_tot % rows == 0
    tab_spec = pl.BlockSpec((rows, LANES), lambda b, t: (t, 0))
    vec_spec = pl.BlockSpec((1, LANES), lambda b, t: (0, 0))

    def x_spec(off):
        return pl.BlockSpec((1, TM, D_MODEL), lambda b, t: (b, jnp.clip(IN_SUB * t - 1 + off, 0, nlt - 1), 0))

    def mod_spec(col, ctx_row):
        return pl.BlockSpec((1, 1, D_MODEL),
                            lambda b, t: (layer * MOD_ROWS + (MOD_ROWS - 8 if ctx_row else b), 0, col))

    return pl.pallas_call(
        _inproj_kernel,
        grid=(bsz, t_tot // rows),
        in_specs=[
            x_spec(0), x_spec(1), x_spec(2),
            pl.BlockSpec((1, TM, D_MODEL), lambda b, t: (b, 0, 0)),
            mod_spec(0, True), mod_spec(1, True), mod_spec(0, False), mod_spec(1, False),
            pl.BlockSpec((D_MODEL, IN_W), lambda b, t: (0, 0), pipeline_mode=pl.Buffered(1)),
            vec_spec, vec_spec,
            tab_spec, tab_spec, tab_spec, tab_spec, tab_spec, tab_spec,
        ],
        out_specs=[
            pl.BlockSpec((1, A_W // LANES, rows, LANES), lambda b, t: (b, 0, t, 0)),
            pl.BlockSpec((1, rows, ATT_W), lambda b, t: (b, t, 0)),
            pl.BlockSpec((1, rows, D_MODEL), lambda b, t: (b, t, 0)),
        ],
        out_shape=[
            jax.ShapeDtypeStruct((bsz, A_W // LANES, t_tot, LANES), F32),
            jax.ShapeDtypeStruct((bsz, t_tot, ATT_W), BF16),
            jax.ShapeDtypeStruct((bsz, t_tot, D_MODEL), BF16),
        ],
        scratch_shapes=[pltpu.VMEM((rows, D_MODEL), BF16)],
        compiler_params=pltpu.CompilerParams(dimension_semantics=("arbitrary", "arbitrary"),
                                             vmem_limit_bytes=56 * 1024 * 1024),
        name="in_proj",
    )(x, x, x, cx, modr, modr, modr, modr, wp, qn, kn, *tabs)


def _hgrn_levels():
    w = HG_CHUNK // 2
    out = []
    while w >= HG_SUB:
        out.append(w)
        w //= 2
    return tuple(out)


def _hgrn_time_of_row():
    nsub = HG_CHUNK // HG_SUB
    r = np.arange(HG_CHUNK)
    return (r % nsub) * HG_SUB + r // nsub


def _hgrn_sum_matrix(rev):
    c = HG_CHUNK
    t = np.arange(c)[:, None]
    j = np.arange(c)[None, :]
    mats = [j <= t, j > t]
    for w in _hgrn_levels():
        m = (t // (2 * w)) * 2 * w + w
        mats.append(np.where(t >= m, (j >= m) & (j <= t), (j > t) & (j <= m - 1)))
    if rev:
        mats = [m[::-1, ::-1] for m in mats]
    tor = _hgrn_time_of_row()
    full = np.concatenate([m[tor][:, tor] for m in mats], axis=0).astype(np.float32)
    return np.concatenate([full, full], axis=1)


def _hgrn_kernel(layer, depth, nc_ctx, za_ref, lb_ref, g_ref, pmf_ref, pmb_ref, y_ref,
                 of_scr, ob_scr, stf_scr, stb_scr):
    c = HG_CHUNK
    u = HG_SUB
    nsub = c // u
    w4 = GROUP_W
    levels = _hgrn_levels()
    t_tot = za_ref.shape[2]
    nc_tot = t_tot // c
    log2e = 1.0 / math.log(2.0)

    lbl = [lb_ref[j] for j in range(depth)]
    mx = functools.reduce(jnp.maximum, lbl)
    ex = [jnp.exp(v - mx) for v in lbl]
    den = functools.reduce(lambda a, b: a + b, ex)
    lower = jnp.zeros_like(den)
    for j in range(1, layer + 1):
        lower = lower + ex[j] / den

    r256 = lax.broadcasted_iota(jnp.int32, (w4, w4), 0)
    c256 = lax.broadcasted_iota(jnp.int32, (w4, w4), 1)
    headmask = (r256 // HEAD_DIM) == (c256 // HEAD_DIM)
    ones_bd = jnp.where(headmask, 1.0, 0.0).astype(BF16)
    rrow = lax.broadcasted_iota(jnp.int32, (c, w4), 0)
    trow = (rrow % nsub) * u + rrow // nsub
    rcol = lax.broadcasted_iota(jnp.int32, (c, w4), 1) % HEAD_DIM
    scol = (rcol % nsub) * u + rcol // nsub

    def load(r0, slab0):
        return jnp.concatenate(
            [jnp.concatenate([za_ref[0, slab0 + j, pl.ds(r0 + p, nsub, stride=u), :] for p in range(u)], axis=0)
             for j in range(2)], axis=1)

    def chunk(r0, rev, st_scr):
        zq = load(r0, 0)
        v = load(r0, 2)
        zf = load(r0, 6 if rev else 4)
        lb = lower[1:2] if rev else lower[0:1]
        q = _silu(zq)
        f = lb + (1.0 - lb) * jax.nn.sigmoid(zf)
        k = 1.0 - f
        lf2 = jnp.log(f) * log2e

        hi, lo = _split2(lf2)
        pm = pmb_ref[...] if rev else pmf_ref[...]
        esum = _dot(pm, jnp.concatenate([hi, lo], axis=0))
        e_cum = esum[0:c]
        e_after = esum[c:2 * c]
        e_tot = e_cum[0:1] if rev else e_cum[c - 1:c]

        st = st_scr[...]
        vb = v.astype(BF16)
        o = _dot((q * jnp.exp2(e_cum)).astype(BF16), st.astype(BF16), NT)

        att = jnp.zeros((c, w4), F32)
        for li, w in enumerate(levels):
            ew = jnp.exp2(esum[(2 + li) * c:(3 + li) * c])
            is_q = ((trow // w) % 2 == 0) if rev else ((trow // w) % 2 == 1)
            qw = jnp.where(is_q, q * ew, 0.0).astype(BF16)
            kw = jnp.where(is_q, 0.0, k * ew).astype(BF16)
            kbd = jnp.where(headmask, jnp.concatenate([kw] * 4, axis=0), jnp.zeros((), BF16))
            a = _dot(qw, kbd, NT)
            att = att + jnp.where((trow // (2 * w)) == (scol // (2 * w)), a, 0.0)
        vbd = jnp.where(headmask, jnp.concatenate([vb] * 4, axis=0), jnp.zeros((), BF16))
        o = o + _dot(att.astype(BF16), vbd)

        grp = lambda a, p: a[p * nsub:(p + 1) * nsub]
        pairs = [(p, s) for p in range(u) for s in range(u) if (s >= p if rev else s <= p)]
        tiles = []
        for p, s in pairs:
            qk = grp(q, p) * grp(k, s)
            tiles.append(qk if p == s else qk * jnp.exp2(grp(e_cum, p) - grp(e_cum, s)))
        red = _dot(jnp.concatenate(tiles, axis=0).astype(BF16), ones_bd)
        ods = [None] * u
        for i, (p, s) in enumerate(pairs):
            term = red[i * nsub:(i + 1) * nsub] * grp(v, s)
            ods[p] = term if ods[p] is None else ods[p] + term
        o = o + jnp.concatenate(ods, axis=0)

        upd = _dot(vb, (k * jnp.exp2(e_after)).astype(BF16), TN)
        st_scr[...] = st * jnp.exp2(e_tot) + jnp.where(headmask, upd, 0.0)
        return o

    def store(o_scr, r0, o):
        for j in range(2):
            for p in range(u):
                o_scr[j, pl.ds(r0 + p, nsub, stride=u), :] = o[p * nsub:(p + 1) * nsub, j * LANES:(j + 1) * LANES]

    stf_scr[...] = jnp.zeros_like(stf_scr)
    stb_scr[...] = jnp.zeros_like(stb_scr)

    def body(i2, carry):
        for jj in range(HG_UNROLL):
            i = i2 * HG_UNROLL + jj
            r0 = pl.multiple_of(i * c, c)
            store(of_scr, r0, chunk(r0, False, stf_scr))
            ci = jnp.where(i < nc_ctx, nc_ctx - 1 - i, nc_tot - 1 - (i - nc_ctx))
            r1 = pl.multiple_of(ci * c, c)
            store(ob_scr, r1, chunk(r1, True, stb_scr))
        return carry

    lax.fori_loop(0, nc_tot // HG_UNROLL, body, 0)

    g = g_ref[...]
    nrow = 4 * c

    def norm_body(i, carry):
        r0 = pl.multiple_of(i * nrow, nrow)
        o = jnp.concatenate([of_scr[j, pl.ds(r0, nrow), :] + ob_scr[j, pl.ds(r0, nrow), :] for j in range(2)], axis=1)
        ms = _dot((o * o).astype(BF16), ones_bd) * (1.0 / HEAD_DIM)
        y_ref[0, pl.ds(r0, nrow), :] = (o * lax.rsqrt(ms + RMS_EPS) * g).astype(y_ref.dtype)
        return carry

    lax.fori_loop(0, t_tot // nrow, norm_body, 0)


def _hgrn_call(za, lb_logits, g4, pmf, pmb, layer, lc):
    bsz, nslab, t_tot, _ = za.shape
    depth = lb_logits.shape[0]
    return pl.pallas_call(
        functools.partial(_hgrn_kernel, layer, depth, lc // HG_CHUNK),
        grid=(bsz,),
        in_specs=[
            pl.BlockSpec((1, nslab, t_tot, LANES), lambda b: (b, 0, 0, 0)),
            pl.BlockSpec((depth, 2, GROUP_W), lambda b: (0, 0, 0)),
            pl.BlockSpec((1, GROUP_W), lambda b: (0, 0)),
            pl.BlockSpec(pmf.shape, lambda b: (0, 0)),
            pl.BlockSpec(pmb.shape, lambda b: (0, 0)),
        ],
        out_specs=pl.BlockSpec((1, t_tot, GROUP_W), lambda b: (b, 0, 0)),
        out_shape=jax.ShapeDtypeStruct((bsz, t_tot, GROUP_W), BF16),
        scratch_shapes=[pltpu.VMEM((2, t_tot, LANES), F32), pltpu.VMEM((2, t_tot, LANES), F32),
                        pltpu.VMEM((GROUP_W, GROUP_W), F32), pltpu.VMEM((GROUP_W, GROUP_W), F32)],
        compiler_params=pltpu.CompilerParams(dimension_semantics=("arbitrary",),
                                             vmem_limit_bytes=48 * 1024 * 1024),
        name="hgrn2",
    )(za, lb_logits, g4, pmf, pmb)


def _lane_iota(rows):
    return lax.broadcasted_iota(jnp.int32, (rows, LANES), 1)


def _mask_q(q, lane, off, width):
    return jnp.where((lane >= off) & (lane < off + width), q, jnp.zeros((), q.dtype))


def _fill_v_ones(vx_scr, v_ref):
    @pl.when(pl.program_id(1) == 0)
    def _():
        vx_scr[:, 0:LANES] = v_ref[0]
        vx_scr[:, LANES:2 * LANES] = jnp.ones((v_ref.shape[1], LANES), BF16)


def _attn_b_kernel(tile0, nct, q_ref, k_ref, v_ref, o_ref, vx_scr, l_scr, e_scr):
    t = pl.program_id(1) + tile0
    lane = _lane_iota(TM)
    t_tot = k_ref.shape[1]
    _fill_v_ones(vx_scr, v_ref)

    def run(nk):
        def logits(hd):
            j, half = divmod(hd, 2)
            q = q_ref[0, :, j * LANES:(j + 1) * LANES]
            l_scr[hd, :, 0:nk] = _dot(_mask_q(q, lane, half * HEAD_DIM, HEAD_DIM), k_ref[0, 0:nk, :], NT)

        def softmax(hd):
            mx = jnp.max(l_scr[hd, :, 0:nk], axis=-1, keepdims=True)
            e_scr[hd, :, 0:nk] = jnp.exp2(l_scr[hd, :, 0:nk] - mx).astype(BF16)

        def values(hd):
            acc = _dot(e_scr[hd, :, 0:nk], vx_scr[0:nk, :])
            return acc[:, 0:LANES] / acc[:, LANES:2 * LANES]

        nh = 4
        outs = [None] * nh
        logits(0)
        for hd in range(nh):
            if hd + 1 < nh:
                logits(hd + 1)
            softmax(hd)
            outs[hd] = values(hd)
        for j in range(2):
            o_ref[0, :, j * LANES:(j + 1) * LANES] = jnp.where(
                lane < HEAD_DIM, outs[2 * j], outs[2 * j + 1]).astype(o_ref.dtype)

    if tile0 < nct:
        @pl.when(t < nct)
        def _():
            run(nct * TM)

    @pl.when(t >= nct)
    def _():
        run(t_tot)


def _attn_c_kernel(tile0, nct, lam_init, q_ref, k_ref, v_ref, lam_ref, g_ref, o_ref, vx0_scr, vx1_scr, l_scr, e_scr):
    t = pl.program_id(1) + tile0
    lane = _lane_iota(TM)
    t_tot = k_ref.shape[1]
    lp = lam_ref[...]
    lam = (jnp.exp(jnp.sum(lp[0:1] * lp[1:2], axis=-1, keepdims=True))
           - jnp.exp(jnp.sum(lp[2:3] * lp[3:4], axis=-1, keepdims=True)) + lam_init)
    ones64 = _block_ones(LANES, HEAD_DIM)
    vx = (vx0_scr, vx1_scr)

    @pl.when(pl.program_id(1) == 0)
    def _():
        for blk in range(2):
            vx[blk][:, 0:LANES] = v_ref[0, :, blk * LANES:(blk + 1) * LANES]
            vx[blk][:, LANES:2 * LANES] = jnp.ones((t_tot, LANES), BF16)

    def run(nk):
        nslot = l_scr.shape[0]

        def logits(grp):
            blk, off = grp // 4, (grp % 4) * DF_QK
            q = q_ref[0, :, blk * LANES:(blk + 1) * LANES]
            l_scr[grp % nslot, :, 0:nk] = _dot(_mask_q(q, lane, off, DF_QK),
                                               k_ref[0, 0:nk, blk * LANES:(blk + 1) * LANES], NT)

        def softmax(grp):
            s = grp % nslot
            mx = jnp.max(l_scr[s, :, 0:nk], axis=-1, keepdims=True)
            e_scr[s, :, 0:nk] = jnp.exp2(l_scr[s, :, 0:nk] - mx).astype(BF16)

        def values(grp):
            acc = _dot(e_scr[grp % nslot, :, 0:nk], vx[grp // 4][0:nk, :])
            return acc[:, 0:LANES] / acc[:, LANES:2 * LANES]

        nmap = 8
        res = [None] * nmap
        logits(0)
        for grp in range(nmap):
            if grp + 1 < nmap:
                logits(grp + 1)
            softmax(grp)
            res[grp] = values(grp)
        outs = [res[2 * hd] - lam * res[2 * hd + 1] for hd in range(4)]
        for hp in range(2):
            o = jnp.where(lane < HEAD_DIM, outs[2 * hp], outs[2 * hp + 1])
            ms = _seg_sum(o * o, ones64) * (1.0 / HEAD_DIM)
            y = o * lax.rsqrt(ms + RMS_EPS) * g_ref[...] * (1.0 - lam_init)
            o_ref[0, :, hp * LANES:(hp + 1) * LANES] = y.astype(o_ref.dtype)

    if tile0 < nct:
        @pl.when(t < nct)
        def _():
            run(nct * TM)

    @pl.when(t >= nct)
    def _():
        run(t_tot)


def _attn_d_kernel(tile0, nct, sink_ref, q_ref, k_ref, v_ref, o_ref, vx_scr, l_scr, e_scr):
    t = pl.program_id(1) + tile0
    lane = _lane_iota(TM)
    lc = nct * TM
    s_lat = k_ref.shape[1] - lc
    win = TM + 2 * WINDOW
    _fill_v_ones(vx_scr, v_ref)

    def run(latent):
        ncol = lc + win if latent else lc
        if latent:
            base = (t - nct) * TM
            wstart = pl.multiple_of(jnp.clip(base - WINDOW, 0, s_lat - win), WINDOW)
            qpos = base + lax.broadcasted_iota(jnp.int32, (TM, win), 0)
            kpos = wstart + lax.broadcasted_iota(jnp.int32, (TM, win), 1)
            allowed = jnp.abs(kpos - qpos) <= WINDOW

        def logits(hd):
            j, half = divmod(hd, 2)
            qm = _mask_q(q_ref[0, :, j * LANES:(j + 1) * LANES], lane, half * HEAD_DIM, HEAD_DIM)
            l_scr[hd, :, 0:lc] = _dot(qm, k_ref[0, 0:lc, :], NT)
            if latent:
                lw = _dot(qm, k_ref[0, pl.ds(lc + wstart, win), :], NT)
                l_scr[hd, :, lc:ncol] = jnp.where(allowed, lw, -jnp.inf)

        def softmax(hd):
            sink = sink_ref[HEAD_PERM[hd]] * LOG2E
            mx = jnp.maximum(jnp.max(l_scr[hd, :, 0:ncol], axis=-1, keepdims=True), sink)
            e_scr[hd, :, 0:ncol] = jnp.exp2(l_scr[hd, :, 0:ncol] - mx).astype(BF16)
            return jnp.exp2(sink - mx)

        def values(hd, e_sink):
            acc = _dot(e_scr[hd, :, 0:lc], vx_scr[0:lc, :])
            if latent:
                acc = acc + _dot(e_scr[hd, :, lc:ncol], vx_scr[pl.ds(lc + wstart, win), :])
            return acc[:, 0:LANES] / (acc[:, LANES:2 * LANES] + e_sink)

        nh = 4
        outs = [None] * nh
        logits(0)
        for hd in range(nh):
            if hd + 1 < nh:
                logits(hd + 1)
            outs[hd] = values(hd, softmax(hd))
        for j in range(2):
            o_ref[0, :, j * LANES:(j + 1) * LANES] = jnp.where(
                lane < HEAD_DIM, outs[2 * j], outs[2 * j + 1]).astype(o_ref.dtype)

    if tile0 < nct:
        @pl.when(t < nct)
        def _():
            run(False)

    @pl.when(t >= nct)
    def _():
        run(True)


def _attn_call(kind, at, lc, ctx_out, extra=(), lam_init=None):
    bsz, t_tot, _ = at.shape
    nct = lc // TM
    ntile = t_tot // TM
    tile0 = 0 if ctx_out else nct
    qb, kb, vb, nkv = {"B": (B_Q, B_K, B_V, 1), "C": (C_Q, C_K, C_V, 2), "D": (D_Q, D_K, D_V, 1)}[kind]
    q_spec = pl.BlockSpec((1, TM, 2 * LANES), lambda b, t: (b, t + tile0, qb // 2))
    k_spec = pl.BlockSpec((1, t_tot, nkv * LANES), lambda b, t: (b, 0, kb // nkv))
    v_spec = pl.BlockSpec((1, t_tot, nkv * LANES), lambda b, t: (b, 0, vb // nkv))
    o_spec = pl.BlockSpec((1, TM, GROUP_W), lambda b, t: (b, t, 0))
    scratch = []
    if kind == "B":
        body = functools.partial(_attn_b_kernel, tile0, nct)
        in_specs, args = [q_spec, k_spec, v_spec], (at, at, at)
        scratch = [pltpu.VMEM((t_tot, 2 * LANES), BF16), pltpu.VMEM((4, TM, t_tot), F32),
                   pltpu.VMEM((4, TM, t_tot), BF16)]
    elif kind == "C":
        body = functools.partial(_attn_c_kernel, tile0, nct, lam_init)
        lam, g2 = extra
        in_specs = [q_spec, k_spec, v_spec,
                    pl.BlockSpec(lam.shape, lambda b, t: (0, 0)), pl.BlockSpec((1, LANES), lambda b, t: (0, 0))]
        args = (at, at, at, lam, g2)
        scratch = [pltpu.VMEM((t_tot, 2 * LANES), BF16), pltpu.VMEM((t_tot, 2 * LANES), BF16),
                   pltpu.VMEM((4, TM, t_tot), F32), pltpu.VMEM((4, TM, t_tot), BF16)]
    else:
        body = functools.partial(_attn_d_kernel, tile0, nct)
        (sink,) = extra
        in_specs = [pl.BlockSpec(memory_space=pltpu.SMEM), q_spec, k_spec, v_spec]
        args = (sink, at, at, at)
        ncol = lc + TM + 2 * WINDOW
        scratch = [pltpu.VMEM((t_tot, 2 * LANES), BF16), pltpu.VMEM((4, TM, ncol), F32),
                   pltpu.VMEM((4, TM, ncol), BF16)]
    return pl.pallas_call(
        body,
        grid=(bsz, ntile - tile0),
        in_specs=in_specs,
        out_specs=o_spec,
        out_shape=jax.ShapeDtypeStruct((bsz, t_tot - tile0 * TM, GROUP_W), BF16),
        scratch_shapes=scratch,
        compiler_params=pltpu.CompilerParams(dimension_semantics=("arbitrary", "arbitrary"),
                                             vmem_limit_bytes=48 * 1024 * 1024),
        name="attn_" + kind.lower(),
    )(*args)


def _outproj_kernel(nct, ctx_out, alpha, ya_ref, yb_ref, yc_ref, yd_ref, sg_ref, x_ref, c_ref, gt_ref,
                    w_ref, lg_ref, lb_ref, *out_refs):
    t = pl.program_id(1) + (0 if ctx_out else nct)
    y = jnp.concatenate([ya_ref[0], yb_ref[0], yc_ref[0], yd_ref[0]], axis=-1) * sg_ref[0]
    p = _dot(y, w_ref[...])
    gate = gt_ref[0]

    def fin(res):
        z = alpha * res + gate * p
        mu = jnp.mean(z, axis=-1, keepdims=True)
        zc = z - mu
        var = jnp.mean(zc * zc, axis=-1, keepdims=True)
        return zc * lax.rsqrt(var + LN_EPS) * lg_ref[...] + lb_ref[...]

    if ctx_out:
        @pl.when(t < nct)
        def _():
            out_refs[1][0] = fin(c_ref[0])

    @pl.when(t >= nct)
    def _():
        out_refs[0][0] = fin(x_ref[0])


def _outproj_call(ys, sg, x, cx, modr, layer, wo, lng, lnb, ctx_out, alpha):
    bsz, s, _ = x.shape
    lc = cx.shape[1]
    nct, nlt = lc // TM, s // TM
    tile0 = 0 if ctx_out else nct

    def y_spec(arr):
        off = tile0 if arr.shape[1] == lc + s else 0
        return pl.BlockSpec((1, TM, GROUP_W), lambda b, t: (b, t + off, 0))

    vec = pl.BlockSpec((1, D_MODEL), lambda b, t: (0, 0))
    in_specs = [y_spec(a) for a in ys] + [
        pl.BlockSpec((1, TM, D_MODEL), lambda b, t: (b, t + tile0, 0)),
        pl.BlockSpec((1, TM, D_MODEL), lambda b, t: (b, jnp.maximum(t + tile0 - nct, 0), 0)),
        pl.BlockSpec((1, TM, D_MODEL), lambda b, t: (b, jnp.minimum(t + tile0, nct - 1), 0)),
        pl.BlockSpec((1, 1, D_MODEL),
                     lambda b, t: (layer * MOD_ROWS + jnp.where(t + tile0 < nct, MOD_ROWS - 8, b), 0, 2)),
        pl.BlockSpec((D_MODEL, D_MODEL), lambda b, t: (0, 0)),
        vec, vec,
    ]
    out_specs = [pl.BlockSpec((1, TM, D_MODEL), lambda b, t: (b, jnp.maximum(t + tile0 - nct, 0), 0))]
    out_shape = [jax.ShapeDtypeStruct((bsz, s, D_MODEL), F32)]
    if ctx_out:
        out_specs.append(pl.BlockSpec((1, TM, D_MODEL), lambda b, t: (b, jnp.minimum(t, nct - 1), 0)))
        out_shape.append(jax.ShapeDtypeStruct((bsz, lc, D_MODEL), F32))
    res = pl.pallas_call(
        functools.partial(_outproj_kernel, nct, ctx_out, alpha),
        grid=(bsz, nct + nlt - tile0),
        in_specs=in_specs,
        out_specs=out_specs,
        out_shape=out_shape,
        compiler_params=pltpu.CompilerParams(dimension_semantics=("arbitrary", "arbitrary"),
                                             vmem_limit_bytes=48 * 1024 * 1024),
        name="out_proj",
    )(*ys, sg, x, cx, modr, wo, lng, lnb)
    return (res[0], res[1]) if ctx_out else (res[0], None)


def _in_perm():
    hp = np.array(HEAD_PERM)
    head_cols = (hp[:, None] * HEAD_DIM + np.arange(HEAD_DIM)[None, :]).reshape(-1)
    idx = np.arange(4 * GROUP_W + 2 * GROUP_W + 3 * GROUP_W + 2 * GROUP_W + D_MODEL)
    b0 = A_W
    d0 = A_W + 2 * GROUP_W + 3 * GROUP_W
    idx[b0:b0 + GROUP_W] = b0 + head_cols
    idx[d0:d0 + GROUP_W] = d0 + head_cols
    idx[GATE_OFF + GROUP_W:GATE_OFF + 2 * GROUP_W] = GATE_OFF + GROUP_W + head_cols
    idx[GATE_OFF + 3 * GROUP_W:GATE_OFF + 4 * GROUP_W] = GATE_OFF + 3 * GROUP_W + head_cols
    return idx


def _out_perm():
    hp = np.array(HEAD_PERM)
    head_rows = (hp[:, None] * HEAD_DIM + np.arange(HEAD_DIM)[None, :]).reshape(-1)
    idx = np.arange(D_MODEL)
    idx[GROUP_W:2 * GROUP_W] = GROUP_W + head_rows
    idx[3 * GROUP_W:4 * GROUP_W] = 3 * GROUP_W + head_rows
    return idx


def _take_runs(a, idx, axis):
    idx = [int(v) for v in idx]
    runs, start = [], idx[0]
    for prev, cur in zip(idx, idx[1:]):
        if cur != prev + 1:
            runs.append((start, prev + 1))
            start = cur
    runs.append((start, idx[-1] + 1))
    return jnp.concatenate([lax.slice_in_dim(a, lo, hi, axis=axis) for lo, hi in runs], axis=axis)


def _rope_tables(n_lat, n_ctx, dim):
    rows = n_lat // GRID_W
    row = jnp.repeat(jnp.arange(rows, dtype=F32), GRID_W)
    col = jnp.broadcast_to(jnp.arange(GRID_W, dtype=F32), (rows, GRID_W)).reshape(-1)
    d_axis = dim // 2
    inv = ROPE_THETA ** (-jnp.arange(0, d_axis, 2, dtype=F32) / d_axis)
    ang_r = row[:, None] * inv
    ang_c = col[:, None] * inv
    ang = jnp.concatenate([ang_r, ang_r, ang_c, ang_c], axis=-1)
    cos, sin = jnp.cos(ang), jnp.sin(ang)
    first = (np.arange(dim) % (dim // 2)) < (dim // 4)
    sa = jnp.where(first[None, :], -sin, 0.0)
    sb = jnp.where(first[None, :], 0.0, sin)
    reps = LANES // dim
    pad = lambda a, fill: jnp.concatenate([jnp.full((n_ctx, LANES), fill, F32), jnp.tile(a, (1, reps))], axis=0)
    return pad(cos, 1.0), pad(sa, 0.0), pad(sb, 0.0)


def kernel(x, c, ctx, c_ctx, w_in, w_out, w_ada, b_ada, ln_g, ln_b, hg_lb_logits, hg_norm_g, ga_q_norm_g,
           ga_k_norm_g, df_lambda, df_subln_g, wn_sink):
    bsz, s, d = x.shape
    lc = ctx.shape[1]
    depth = w_in.shape[0]
    assert d == D_MODEL and s % TM == 0 and lc % TM == 0 and s >= TM + 2 * WINDOW and bsz <= MOD_ROWS - 8
    assert w_in.shape[2] == IN_W

    cc = jnp.concatenate([c, jnp.zeros((MOD_ROWS - 8 - bsz, d), F32), c_ctx[None, :],
                          jnp.zeros((7, d), F32)], axis=0)
    mod = _ada_call(cc, w_ada, b_ada)
    modr = mod.reshape(depth * MOD_ROWS, 1, 3 * d)

    tabs = _rope_tables(s, lc, HEAD_DIM) + _rope_tables(s, lc, DF_QK)
    wp = _take_runs(w_in.astype(BF16), _in_perm(), 2)
    wo = _take_runs(w_out.astype(BF16), _out_perm(), 1)
    pmf = jnp.asarray(_hgrn_sum_matrix(False), BF16)
    pmb = jnp.asarray(_hgrn_sum_matrix(True), BF16)
    tile2 = lambda v: jnp.tile(v, 2)[None, :]
    alpha = (2.0 * depth) ** 0.25

    cx = ctx
    for l in range(depth):
        ctx_out = l < depth - 1
        za, at, sg = _inproj_call(x, cx, modr, l, wp[l], tile2(ga_q_norm_g[l]), tile2(ga_k_norm_g[l]), tabs)
        ya = _hgrn_call(za, hg_lb_logits, jnp.tile(hg_norm_g[l], 4)[None, :], pmf, pmb, l, lc)
        lam_init = 0.8 - 0.6 * math.exp(-0.3 * l)
        yb = _attn_call("B", at, lc, ctx_out)
        yc = _attn_call("C", at, lc, ctx_out, extra=(df_lambda[l], tile2(df_subln_g[l])), lam_init=lam_init)
        yd = _attn_call("D", at, lc, ctx_out, extra=(wn_sink[l],))
        x, cx_new = _outproj_call((ya, yb, yc, yd), sg, x, cx, modr, l, wo[l],
                                  ln_g[l][None, :], ln_b[l][None, :], ctx_out, alpha)
        cx = cx_new if ctx_out else cx
    return x
```

```python
import functools
import math

import numpy as np
import jax
import jax.numpy as jnp
from jax import lax
from jax.experimental import pallas as pl
from jax.experimental.pallas import tpu as pltpu

F32 = jnp.float32
BF16 = jnp.bfloat16

D_MODEL = 1024
GROUP_W = 256
HEAD_DIM = 64
DF_QK = 32
GRID_W = 64
WINDOW = 128
ROPE_THETA = 10000.0
LN_EPS = 1e-5
RMS_EPS = 1e-6
LOG2E = 1.0 / math.log(2.0)
HG_CHUNK = 64
HG_SUB = 8
HG_UNROLL = 6

LANES = 128
TM = 256
IN_SUB = 3
MOD_ROWS = 24

A_W = 4 * GROUP_W
ATT_W = 14 * LANES
ATT_OFF = A_W
GATE_OFF = A_W + ATT_W
IN_W = GATE_OFF + D_MODEL
B_Q, B_K, B_V = 0, 2, 3
C_Q, C_K, C_V = 4, 6, 8
D_Q, D_K, D_V = 10, 12, 13
HEAD_PERM = (0, 2, 1, 3)

NT = (((1,), (1,)), ((), ()))
TN = (((0,), (0,)), ((), ()))


def _dot(a, b, dims=None):
    if dims is None:
        return jnp.dot(a, b, preferred_element_type=F32)
    return lax.dot_general(a, b, dims, preferred_element_type=F32)


def _split2(x):
    hi = x.astype(BF16)
    lo = (x - hi.astype(F32)).astype(BF16)
    return hi, lo


def _seg_sum(x, ones_bd):
    hi, lo = _split2(x)
    return _dot(hi, ones_bd) + _dot(lo, ones_bd)


def _block_ones(n, seg):
    r = lax.broadcasted_iota(jnp.int32, (n, n), 0) // seg
    c = lax.broadcasted_iota(jnp.int32, (n, n), 1) // seg
    return jnp.where(r == c, 1.0, 0.0).astype(BF16)


def _silu(x):
    return x * jax.nn.sigmoid(x)


def _ada_kernel(c_ref, w_ref, b_ref, o_ref):
    s = _silu(c_ref[...])
    o_ref[0] = jnp.dot(s, w_ref[0], preferred_element_type=F32, precision=lax.Precision.HIGHEST) + b_ref[0]


def _ada_call(cc, w_ada, b_ada):
    depth = w_ada.shape[0]
    n3 = w_ada.shape[2]
    nb = n3 // D_MODEL
    return pl.pallas_call(
        _ada_kernel,
        grid=(depth, nb),
        in_specs=[
            pl.BlockSpec((MOD_ROWS, D_MODEL), lambda l, j: (0, 0)),
            pl.BlockSpec((1, D_MODEL, D_MODEL), lambda l, j: (l, 0, j)),
            pl.BlockSpec((1, 1, D_MODEL), lambda l, j: (l, 0, j)),
        ],
        out_specs=pl.BlockSpec((1, MOD_ROWS, D_MODEL), lambda l, j: (l, 0, j)),
        out_shape=jax.ShapeDtypeStruct((depth, MOD_ROWS, n3), F32),
        compiler_params=pltpu.CompilerParams(dimension_semantics=("arbitrary", "arbitrary"),
                                             vmem_limit_bytes=32 * 1024 * 1024),
        name="ada_mod",
    )(cc, w_ada, b_ada.reshape(depth, 1, n3))


def _rope(x, cos, sa, sb, shift):
    up = pltpu.roll(x, LANES - shift, axis=1)
    dn = pltpu.roll(x, shift, axis=1)
    return x * cos + up * sa + dn * sb


def _inproj_kernel(xa_ref, xb_ref, xc_ref, c_ref, shc_ref, scc_ref, sh_ref, sc_ref, w_ref, qn_ref, kn_ref,
                   c64_ref, sa64_ref, sb64_ref, c32_ref, sa32_ref, sb32_ref,
                   za_ref, at_ref, sg_ref, h_scr):
    t = pl.program_id(1)
    sc = 1.0 + sc_ref[0]
    sh = sh_ref[0]

    @pl.when(t == 0)
    def _():
        h_scr[0:TM, :] = (c_ref[0] * (1.0 + scc_ref[0]) + shc_ref[0]).astype(BF16)

    @pl.when(t > 0)
    def _():
        h_scr[0:TM, :] = (xa_ref[0] * sc + sh).astype(BF16)

    h_scr[TM:2 * TM, :] = (xb_ref[0] * sc + sh).astype(BF16)
    h_scr[2 * TM:3 * TM, :] = (xc_ref[0] * sc + sh).astype(BF16)
    h = h_scr[...]

    def proj(blk0, nblk):
        c0 = ATT_OFF + blk0 * LANES
        return _dot(h, w_ref[:, c0:c0 + nblk * LANES])

    def put(blk, val):
        at_ref[0, :, blk * LANES:(blk + 1) * LANES] = val.astype(BF16)

    def lanes(a, j):
        return a[:, j * LANES:(j + 1) * LANES]

    acc = _dot(h, w_ref[:, 0:A_W])
    for j in range(A_W // LANES):
        za_ref[0, j] = lanes(acc, j)

    ones64 = _block_ones(LANES, HEAD_DIM)
    c64, sa64, sb64 = c64_ref[...], sa64_ref[...], sb64_ref[...]
    c32, sa32, sb32 = c32_ref[...], sa32_ref[...], sb32_ref[...]
    q64 = HEAD_DIM // 4
    q32 = DF_QK // 4

    def rms(v, g):
        ms = _seg_sum(v * v, ones64) * (1.0 / HEAD_DIM)
        return v * lax.rsqrt(ms + RMS_EPS) * g

    acc = proj(B_Q, 4)
    for j in range(2):
        put(B_Q + j, _rope(rms(lanes(acc, j), qn_ref[...]), c64, sa64, sb64, q64) * (HEAD_DIM ** -0.5 * LOG2E))
    put(B_K, _rope(rms(lanes(acc, 2), kn_ref[...]), c64, sa64, sb64, q64))
    put(B_V, lanes(acc, 3))

    acc = proj(C_Q, 6)
    for j in range(2):
        put(C_Q + j, _rope(lanes(acc, j), c32, sa32, sb32, q32) * (DF_QK ** -0.5 * LOG2E))
        put(C_K + j, _rope(lanes(acc, 2 + j), c32, sa32, sb32, q32))
        put(C_V + j, lanes(acc, 4 + j))

    acc = proj(D_Q, 4)
    for j in range(2):
        put(D_Q + j, _rope(lanes(acc, j), c64, sa64, sb64, q64) * (HEAD_DIM ** -0.5 * LOG2E))
    put(D_K, _rope(lanes(acc, 2), c64, sa64, sb64, q64))
    put(D_V, lanes(acc, 3))

    sg_ref[0] = _silu(_dot(h, w_ref[:, GATE_OFF:IN_W])).astype(sg_ref.dtype)


def _inproj_call(x, cx, modr, layer, wp, qn, kn, tabs):
    bsz, s, _ = x.shape
    lc = cx.shape[1]
    nlt = s // TM
    t_tot = lc + s
    rows = IN_SUB * TM
    assert lc == TM and t_tot % rows == 0
    tab_spec = pl.BlockSpec((rows, LANES), lambda b, t: (t, 0))
    vec_spec = pl.BlockSpec((1, LANES), lambda b, t: (0, 0))

    def x_spec(off):
        return pl.BlockSpec((1, TM, D_MODEL), lambda b, t: (b, jnp.clip(IN_SUB * t - 1 + off, 0, nlt - 1), 0))

    def mod_spec(col, ctx_row):
        return pl.BlockSpec((1, 1, D_MODEL),
                            lambda b, t: (layer * MOD_ROWS + (MOD_ROWS - 8 if ctx_row else b), 0, col))

    return pl.pallas_call(
        _inproj_kernel,
        grid=(bsz, t_tot // rows),
        in_specs=[
            x_spec(0), x_spec(1), x_spec(2),
            pl.BlockSpec((1, TM, D_MODEL), lambda b, t: (b, 0, 0)),
            mod_spec(0, True), mod_spec(1, True), mod_spec(0, False), mod_spec(1, False),
            pl.BlockSpec((D_MODEL, IN_W), lambda b, t: (0, 0), pipeline_mode=pl.Buffered(1)),
            vec_spec, vec_spec,
            tab_spec, tab_spec, tab_spec, tab_spec, tab_spec, tab_spec,
        ],
        out_specs=[
            pl.BlockSpec((1, A_W // LANES, rows, LANES), lambda b, t: (b, 0, t, 0)),
            pl.BlockSpec((1, rows, ATT_W), lambda b, t: (b, t, 0)),
            pl.BlockSpec((1, rows, D_MODEL), lambda b, t: (b, t, 0)),
        ],
        out_shape=[
            jax.ShapeDtypeStruct((bsz, A_W // LANES, t_tot, LANES), F32),
            jax.ShapeDtypeStruct((bsz, t_tot, ATT_W), BF16),
            jax.ShapeDtypeStruct((bsz, t_tot, D_MODEL), BF16),
        ],
        scratch_shapes=[pltpu.VMEM((rows, D_MODEL), BF16)],
        compiler_params=pltpu.CompilerParams(dimension_semantics=("arbitrary", "arbitrary"),
                                             vmem_limit_bytes=56 * 1024 * 1024),
        name="in_proj",
    )(x, x, x, cx, modr, modr, modr, modr, wp, qn, kn, *tabs)


def _hgrn_levels():
    w = HG_CHUNK // 2
    out = []
    while w >= HG_SUB:
        out.append(w)
        w //= 2
    return tuple(out)


def _hgrn_time_of_row():
    nsub = HG_CHUNK // HG_SUB
    r = np.arange(HG_CHUNK)
    return (r % nsub) * HG_SUB + r // nsub


def _hgrn_sum_matrix(rev):
    c = HG_CHUNK
    t = np.arange(c)[:, None]
    j = np.arange(c)[None, :]
    mats = [j <= t, j > t]
    for w in _hgrn_levels():
        m = (t // (2 * w)) * 2 * w + w
        mats.append(np.where(t >= m, (j >= m) & (j <= t), (j > t) & (j <= m - 1)))
    if rev:
        mats = [m[::-1, ::-1] for m in mats]
    tor = _hgrn_time_of_row()
    full = np.concatenate([m[tor][:, tor] for m in mats], axis=0).astype(np.float32)
    return np.concatenate([full, full], axis=1)


def _hgrn_kernel(layer, depth, nc_ctx, za_ref, lb_ref, g_ref, pmf_ref, pmb_ref, y_ref,
                 of_scr, ob_scr, stf_scr, stb_scr):
    c = HG_CHUNK
    u = HG_SUB
    nsub = c // u
    w4 = GROUP_W
    levels = _hgrn_levels()
    t_tot = za_ref.shape[2]
    nc_tot = t_tot // c
    log2e = 1.0 / math.log(2.0)

    lbl = [lb_ref[j] for j in range(depth)]
    mx = functools.reduce(jnp.maximum, lbl)
    ex = [jnp.exp(v - mx) for v in lbl]
    den = functools.reduce(lambda a, b: a + b, ex)
    lower = jnp.zeros_like(den)
    for j in range(1, layer + 1):
        lower = lower + ex[j] / den

    r256 = lax.broadcasted_iota(jnp.int32, (w4, w4), 0)
    c256 = lax.broadcasted_iota(jnp.int32, (w4, w4), 1)
    headmask = (r256 // HEAD_DIM) == (c256 // HEAD_DIM)
    ones_bd = jnp.where(headmask, 1.0, 0.0).astype(BF16)
    rrow = lax.broadcasted_iota(jnp.int32, (c, w4), 0)
    trow = (rrow % nsub) * u + rrow // nsub
    rcol = lax.broadcasted_iota(jnp.int32, (c, w4), 1) % HEAD_DIM
    scol = (rcol % nsub) * u + rcol // nsub

    def load(r0, slab0):
        return jnp.concatenate(
            [jnp.concatenate([za_ref[0, slab0 + j, pl.ds(r0 + p, nsub, stride=u), :] for p in range(u)], axis=0)
             for j in range(2)], axis=1)

    def chunk(r0, rev, st_scr):
        zq = load(r0, 0)
        v = load(r0, 2)
        zf = load(r0, 6 if rev else 4)
        lb = lower[1:2] if rev else lower[0:1]
        q = _silu(zq)
        f = lb + (1.0 - lb) * jax.nn.sigmoid(zf)
        k = 1.0 - f
        lf2 = jnp.log(f) * log2e

        hi, lo = _split2(lf2)
        pm = pmb_ref[...] if rev else pmf_ref[...]
        esum = _dot(pm, jnp.concatenate([hi, lo], axis=0))
        e_cum = esum[0:c]
        e_after = esum[c:2 * c]
        e_tot = e_cum[0:1] if rev else e_cum[c - 1:c]

        st = st_scr[...]
        vb = v.astype(BF16)
        o = _dot((q * jnp.exp2(e_cum)).astype(BF16), st.astype(BF16), NT)

        att = jnp.zeros((c, w4), F32)
        for li, w in enumerate(levels):
            ew = jnp.exp2(esum[(2 + li) * c:(3 + li) * c])
            is_q = ((trow // w) % 2 == 0) if rev else ((trow // w) % 2 == 1)
            qw = jnp.where(is_q, q * ew, 0.0).astype(BF16)
            kw = jnp.where(is_q, 0.0, k * ew).astype(BF16)
            kbd = jnp.where(headmask, jnp.concatenate([kw] * 4, axis=0), jnp.zeros((), BF16))
            a = _dot(qw, kbd, NT)
            att = att + jnp.where((trow // (2 * w)) == (scol // (2 * w)), a, 0.0)
        vbd = jnp.where(headmask, jnp.concatenate([vb] * 4, axis=0), jnp.zeros((), BF16))
        o = o + _dot(att.astype(BF16), vbd)

        grp = lambda a, p: a[p * nsub:(p + 1) * nsub]
        pairs = [(p, s) for p in range(u) for s in range(u) if (s >= p if rev else s <= p)]
        tiles = []
        for p, s in pairs:
            qk = grp(q, p) * grp(k, s)
            tiles.append(qk if p == s else qk * jnp.exp2(grp(e_cum, p) - grp(e_cum, s)))
        red = _dot(jnp.concatenate(tiles, axis=0).astype(BF16), ones_bd)
        ods = [None] * u
        for i, (p, s) in enumerate(pairs):
            term = red[i * nsub:(i + 1) * nsub] * grp(v, s)
            ods[p] = term if ods[p] is None else ods[p] + term
        o = o + jnp.concatenate(ods, axis=0)

        upd = _dot(vb, (k * jnp.exp2(e_after)).astype(BF16), TN)
        st_scr[...] = st * jnp.exp2(e_tot) + jnp.where(headmask, upd, 0.0)
        return o

    def store(o_scr, r0, o):
        for j in range(2):
            for p in range(u):
                o_scr[j, pl.ds(r0 + p, nsub, stride=u), :] = o[p * nsub:(p + 1) * nsub, j * LANES:(j + 1) * LANES]

    stf_scr[...] = jnp.zeros_like(stf_scr)
    stb_scr[...] = jnp.zeros_like(stb_scr)

    def body(i2, carry):
        for jj in range(HG_UNROLL):
            i = i2 * HG_UNROLL + jj
            r0 = pl.multiple_of(i * c, c)
            store(of_scr, r0, chunk(r0, False, stf_scr))
            ci = jnp.where(i < nc_ctx, nc_ctx - 1 - i, nc_tot - 1 - (i - nc_ctx))
            r1 = pl.multiple_of(ci * c, c)
            store(ob_scr, r1, chunk(r1, True, stb_scr))
        return carry

    lax.fori_loop(0, nc_tot // HG_UNROLL, body, 0)

    g = g_ref[...]
    nrow = 4 * c

    def norm_body(i, carry):
        r0 = pl.multiple_of(i * nrow, nrow)
        o = jnp.concatenate([of_scr[j, pl.ds(r0, nrow), :] + ob_scr[j, pl.ds(r0, nrow), :] for j in range(2)], axis=1)
        ms = _dot((o * o).astype(BF16), ones_bd) * (1.0 / HEAD_DIM)
        y_ref[0, pl.ds(r0, nrow), :] = (o * lax.rsqrt(ms + RMS_EPS) * g).astype(y_ref.dtype)
        return carry

    lax.fori_loop(0, t_tot // nrow, norm_body, 0)


def _hgrn_call(za, lb_logits, g4, pmf, pmb, layer, lc):
    bsz, nslab, t_tot, _ = za.shape
    depth = lb_logits.shape[0]
    return pl.pallas_call(
        functools.partial(_hgrn_kernel, layer, depth, lc // HG_CHUNK),
        grid=(bsz,),
        in_specs=[
            pl.BlockSpec((1, nslab, t_tot, LANES), lambda b: (b, 0, 0, 0)),
            pl.BlockSpec((depth, 2, GROUP_W), lambda b: (0, 0, 0)),
            pl.BlockSpec((1, GROUP_W), lambda b: (0, 0)),
            pl.BlockSpec(pmf.shape, lambda b: (0, 0)),
            pl.BlockSpec(pmb.shape, lambda b: (0, 0)),
        ],
        out_specs=pl.BlockSpec((1, t_tot, GROUP_W), lambda b: (b, 0, 0)),
        out_shape=jax.ShapeDtypeStruct((bsz, t_tot, GROUP_W), BF16),
        scratch_shapes=[pltpu.VMEM((2, t_tot, LANES), F32), pltpu.VMEM((2, t_tot, LANES), F32),
                        pltpu.VMEM((GROUP_W, GROUP_W), F32), pltpu.VMEM((GROUP_W, GROUP_W), F32)],
        compiler_params=pltpu.CompilerParams(dimension_semantics=("arbitrary",),
                                             vmem_limit_bytes=48 * 1024 * 1024),
        name="hgrn2",
    )(za, lb_logits, g4, pmf, pmb)


def _lane_iota(rows):
    return lax.broadcasted_iota(jnp.int32, (rows, LANES), 1)


def _mask_q(q, lane, off, width):
    return jnp.where((lane >= off) & (lane < off + width), q, jnp.zeros((), q.dtype))


def _attn_kernel(tile0, nct, lam_init, sink_ref, q_ref, kv_ref, lam_ref, g_ref, ob_ref, oc_ref, od_ref,
                 vxb_scr, vxc0_scr, vxc1_scr, vxd_scr, l_scr, e_scr):
    t = pl.program_id(1) + tile0
    lane = _lane_iota(TM)
    t_tot = kv_ref.shape[1]
    lc = nct * TM
    win = TM + 2 * WINDOW
    nslot = l_scr.shape[0]
    lp = lam_ref[...]
    lam = (jnp.exp(jnp.sum(lp[0:1] * lp[1:2], axis=-1, keepdims=True))
           - jnp.exp(jnp.sum(lp[2:3] * lp[3:4], axis=-1, keepdims=True)) + lam_init)
    ones64 = _block_ones(LANES, HEAD_DIM)

    def blk(ref, b, rows=slice(None)):
        return ref[0, rows, b * LANES:(b + 1) * LANES]

    @pl.when(pl.program_id(1) == 0)
    def _():
        for scr, b in ((vxb_scr, B_V), (vxc0_scr, C_V), (vxc1_scr, C_V + 1), (vxd_scr, D_V)):
            scr[:, 0:LANES] = blk(kv_ref, b)
            scr[:, LANES:2 * LANES] = jnp.ones((t_tot, LANES), BF16)

    def run(latent):
        nk = t_tot if latent else lc
        ncol_d = lc + win if latent else lc
        if latent:
            base = (t - nct) * TM
            wstart = pl.multiple_of(jnp.clip(base - WINDOW, 0, t_tot - lc - win), WINDOW)
            qpos = base + lax.broadcasted_iota(jnp.int32, (TM, win), 0)
            kpos = wstart + lax.broadcasted_iota(jnp.int32, (TM, win), 1)
            allowed = jnp.abs(kpos - qpos) <= WINDOW

        def plain_softmax(s, ncol):
            mx = jnp.max(l_scr[s, :, 0:ncol], axis=-1, keepdims=True)
            e_scr[s, :, 0:ncol] = jnp.exp2(l_scr[s, :, 0:ncol] - mx).astype(BF16)

        def ratio(acc, extra=0.0):
            return acc[:, 0:LANES] / (acc[:, LANES:2 * LANES] + extra)

        def stage_b(hd):
            j, half = divmod(hd, 2)

            def logits(s):
                qm = _mask_q(blk(q_ref, B_Q + j), lane, half * HEAD_DIM, HEAD_DIM)
                l_scr[s, :, 0:nk] = _dot(qm, blk(kv_ref, B_K, slice(0, nk)), NT)

            return logits, lambda s: plain_softmax(s, nk), lambda s, _: ratio(_dot(e_scr[s, :, 0:nk], vxb_scr[0:nk, :]))

        def stage_c(grp):
            b, off = grp // 4, (grp % 4) * DF_QK
            vx = vxc0_scr if grp < 4 else vxc1_scr

            def logits(s):
                qm = _mask_q(blk(q_ref, C_Q + b), lane, off, DF_QK)
                l_scr[s, :, 0:nk] = _dot(qm, blk(kv_ref, C_K + b, slice(0, nk)), NT)

            return logits, lambda s: plain_softmax(s, nk), lambda s, _: ratio(_dot(e_scr[s, :, 0:nk], vx[0:nk, :]))

        def stage_d(hd):
            j, half = divmod(hd, 2)

            def logits(s):
                qm = _mask_q(blk(q_ref, D_Q + j), lane, half * HEAD_DIM, HEAD_DIM)
                l_scr[s, :, 0:lc] = _dot(qm, blk(kv_ref, D_K, slice(0, lc)), NT)
                if latent:
                    lw = _dot(qm, blk(kv_ref, D_K, pl.ds(lc + wstart, win)), NT)
                    l_scr[s, :, lc:ncol_d] = jnp.where(allowed, lw, -jnp.inf)

            def softmax(s):
                sink = sink_ref[HEAD_PERM[hd]] * LOG2E
                mx = jnp.maximum(jnp.max(l_scr[s, :, 0:ncol_d], axis=-1, keepdims=True), sink)
                e_scr[s, :, 0:ncol_d] = jnp.exp2(l_scr[s, :, 0:ncol_d] - mx).astype(BF16)
                return jnp.exp2(sink - mx)

            def values(s, e_sink):
                acc = _dot(e_scr[s, :, 0:lc], vxd_scr[0:lc, :])
                if latent:
                    acc = acc + _dot(e_scr[s, :, lc:ncol_d], vxd_scr[pl.ds(lc + wstart, win), :])
                return ratio(acc, e_sink)

            return logits, softmax, values

        stages = [stage_b(i) for i in range(4)] + [stage_c(i) for i in range(8)] + [stage_d(i) for i in range(4)]
        res = []
        stages[0][0](0)
        for i, (_, softmax, values) in enumerate(stages):
            if i + 1 < len(stages):
                stages[i + 1][0]((i + 1) % nslot)
            res.append(values(i % nslot, softmax(i % nslot)))
        rb, rc, rd = res[0:4], res[4:12], res[12:16]

        def pair(a, b):
            return jnp.where(lane < HEAD_DIM, a, b)

        for j in range(2):
            ob_ref[0, :, j * LANES:(j + 1) * LANES] = pair(rb[2 * j], rb[2 * j + 1]).astype(ob_ref.dtype)
            od_ref[0, :, j * LANES:(j + 1) * LANES] = pair(rd[2 * j], rd[2 * j + 1]).astype(od_ref.dtype)
            hc = [rc[2 * hd] - lam * rc[2 * hd + 1] for hd in (2 * j, 2 * j + 1)]
            o = pair(hc[0], hc[1])
            ms = _seg_sum(o * o, ones64) * (1.0 / HEAD_DIM)
            y = o * lax.rsqrt(ms + RMS_EPS) * g_ref[...] * (1.0 - lam_init)
            oc_ref[0, :, j * LANES:(j + 1) * LANES] = y.astype(oc_ref.dtype)

    if tile0 < nct:
        @pl.when(t < nct)
        def _():
            run(False)

    @pl.when(t >= nct)
    def _():
        run(True)


def _attn_bcd_call(at, lc, ctx_out, sink, lam, g2, lam_init):
    bsz, t_tot, _ = at.shape
    nct = lc // TM
    ntile = t_tot // TM
    tile0 = 0 if ctx_out else nct
    o_spec = pl.BlockSpec((1, TM, GROUP_W), lambda b, t: (b, t, 0))
    o_shape = jax.ShapeDtypeStruct((bsz, t_tot - tile0 * TM, GROUP_W), BF16)
    vx = pltpu.VMEM((t_tot, 2 * LANES), BF16)
    return pl.pallas_call(
        functools.partial(_attn_kernel, tile0, nct, lam_init),
        grid=(bsz, ntile - tile0),
        in_specs=[
            pl.BlockSpec(memory_space=pltpu.SMEM),
            pl.BlockSpec((1, TM, ATT_W), lambda b, t: (b, t + tile0, 0)),
            pl.BlockSpec((1, t_tot, ATT_W), lambda b, t: (b, 0, 0)),
            pl.BlockSpec(lam.shape, lambda b, t: (0, 0)),
            pl.BlockSpec((1, LANES), lambda b, t: (0, 0)),
        ],
        out_specs=[o_spec, o_spec, o_spec],
        out_shape=[o_shape, o_shape, o_shape],
        scratch_shapes=[vx, vx, vx, vx, pltpu.VMEM((4, TM, t_tot), F32), pltpu.VMEM((4, TM, t_tot), BF16)],
        compiler_params=pltpu.CompilerParams(dimension_semantics=("arbitrary", "arbitrary"),
                                             vmem_limit_bytes=56 * 1024 * 1024),
        name="attn_bcd",
    )(sink, at, at, lam, g2)


def _outproj_kernel(nct, ctx_out, alpha, ya_ref, yb_ref, yc_ref, yd_ref, sg_ref, x_ref, c_ref, gt_ref,
                    w_ref, lg_ref, lb_ref, *out_refs):
    t = pl.program_id(1) + (0 if ctx_out else nct)
    y = jnp.concatenate([ya_ref[0], yb_ref[0], yc_ref[0], yd_ref[0]], axis=-1) * sg_ref[0]
    p = _dot(y, w_ref[...])
    gate = gt_ref[0]

    def fin(res):
        z = alpha * res + gate * p
        mu = jnp.mean(z, axis=-1, keepdims=True)
        zc = z - mu
        var = jnp.mean(zc * zc, axis=-1, keepdims=True)
        return zc * lax.rsqrt(var + LN_EPS) * lg_ref[...] + lb_ref[...]

    if ctx_out:
        @pl.when(t < nct)
        def _():
            out_refs[1][0] = fin(c_ref[0])

    @pl.when(t >= nct)
    def _():
        out_refs[0][0] = fin(x_ref[0])


def _outproj_call(ys, sg, x, cx, modr, layer, wo, lng, lnb, ctx_out, alpha):
    bsz, s, _ = x.shape
    lc = cx.shape[1]
    nct, nlt = lc // TM, s // TM
    tile0 = 0 if ctx_out else nct

    def y_spec(arr):
        off = tile0 if arr.shape[1] == lc + s else 0
        return pl.BlockSpec((1, TM, GROUP_W), lambda b, t: (b, t + off, 0))

    vec = pl.BlockSpec((1, D_MODEL), lambda b, t: (0, 0))
    in_specs = [y_spec(a) for a in ys] + [
        pl.BlockSpec((1, TM, D_MODEL), lambda b, t: (b, t + tile0, 0)),
        pl.BlockSpec((1, TM, D_MODEL), lambda b, t: (b, jnp.maximum(t + tile0 - nct, 0), 0)),
        pl.BlockSpec((1, TM, D_MODEL), lambda b, t: (b, jnp.minimum(t + tile0, nct - 1), 0)),
        pl.BlockSpec((1, 1, D_MODEL),
                     lambda b, t: (layer * MOD_ROWS + jnp.where(t + tile0 < nct, MOD_ROWS - 8, b), 0, 2)),
        pl.BlockSpec((D_MODEL, D_MODEL), lambda b, t: (0, 0)),
        vec, vec,
    ]
    out_specs = [pl.BlockSpec((1, TM, D_MODEL), lambda b, t: (b, jnp.maximum(t + tile0 - nct, 0), 0))]
    out_shape = [jax.ShapeDtypeStruct((bsz, s, D_MODEL), F32)]
    if ctx_out:
        out_specs.append(pl.BlockSpec((1, TM, D_MODEL), lambda b, t: (b, jnp.minimum(t, nct - 1), 0)))
        out_shape.append(jax.ShapeDtypeStruct((bsz, lc, D_MODEL), F32))
    res = pl.pallas_call(
        functools.partial(_outproj_kernel, nct, ctx_out, alpha),
        grid=(bsz, nct + nlt - tile0),
        in_specs=in_specs,
        out_specs=out_specs,
        out_shape=out_shape,
        compiler_params=pltpu.CompilerParams(dimension_semantics=("arbitrary", "arbitrary"),
                                             vmem_limit_bytes=48 * 1024 * 1024),
        name="out_proj",
    )(*ys, sg, x, cx, modr, wo, lng, lnb)
    return (res[0], res[1]) if ctx_out else (res[0], None)


def _in_perm():
    hp = np.array(HEAD_PERM)
    head_cols = (hp[:, None] * HEAD_DIM + np.arange(HEAD_DIM)[None, :]).reshape(-1)
    idx = np.arange(4 * GROUP_W + 2 * GROUP_W + 3 * GROUP_W + 2 * GROUP_W + D_MODEL)
    b0 = A_W
    d0 = A_W + 2 * GROUP_W + 3 * GROUP_W
    idx[b0:b0 + GROUP_W] = b0 + head_cols
    idx[d0:d0 + GROUP_W] = d0 + head_cols
    idx[GATE_OFF + GROUP_W:GATE_OFF + 2 * GROUP_W] = GATE_OFF + GROUP_W + head_cols
    idx[GATE_OFF + 3 * GROUP_W:GATE_OFF + 4 * GROUP_W] = GATE_OFF + 3 * GROUP_W + head_cols
    return idx


def _out_perm():
    hp = np.array(HEAD_PERM)
    head_rows = (hp[:, None] * HEAD_DIM + np.arange(HEAD_DIM)[None, :]).reshape(-1)
    idx = np.arange(D_MODEL)
    idx[GROUP_W:2 * GROUP_W] = GROUP_W + head_rows
    idx[3 * GROUP_W:4 * GROUP_W] = 3 * GROUP_W + head_rows
    return idx


def _take_runs(a, idx, axis):
    idx = [int(v) for v in idx]
    runs, start = [], idx[0]
    for prev, cur in zip(idx, idx[1:]):
        if cur != prev + 1:
            runs.append((start, prev + 1))
            start = cur
    runs.append((start, idx[-1] + 1))
    return jnp.concatenate([lax.slice_in_dim(a, lo, hi, axis=axis) for lo, hi in runs], axis=axis)


def _rope_tables(n_lat, n_ctx, dim):
    rows = n_lat // GRID_W
    row = jnp.repeat(jnp.arange(rows, dtype=F32), GRID_W)
    col = jnp.broadcast_to(jnp.arange(GRID_W, dtype=F32), (rows, GRID_W)).reshape(-1)
    d_axis = dim // 2
    inv = ROPE_THETA ** (-jnp.arange(0, d_axis, 2, dtype=F32) / d_axis)
    ang_r = row[:, None] * inv
    ang_c = col[:, None] * inv
    ang = jnp.concatenate([ang_r, ang_r, ang_c, ang_c], axis=-1)
    cos, sin = jnp.cos(ang), jnp.sin(ang)
    first = (np.arange(dim) % (dim // 2)) < (dim // 4)
    sa = jnp.where(first[None, :], -sin, 0.0)
    sb = jnp.where(first[None, :], 0.0, sin)
    reps = LANES // dim
    pad = lambda a, fill: jnp.concatenate([jnp.full((n_ctx, LANES), fill, F32), jnp.tile(a, (1, reps))], axis=0)
    return pad(cos, 1.0), pad(sa, 0.0), pad(sb, 0.0)


def kernel(x, c, ctx, c_ctx, w_in, w_out, w_ada, b_ada, ln_g, ln_b, hg_lb_logits, hg_norm_g, ga_q_norm_g,
           ga_k_norm_g, df_lambda, df_subln_g, wn_sink):
    bsz, s, d = x.shape
    lc = ctx.shape[1]
    depth = w_in.shape[0]
    assert d == D_MODEL and s % TM == 0 and lc % TM == 0 and s >= TM + 2 * WINDOW and bsz <= MOD_ROWS - 8
    assert w_in.shape[2] == IN_W

    cc = jnp.concatenate([c, jnp.zeros((MOD_ROWS - 8 - bsz, d), F32), c_ctx[None, :],
                          jnp.zeros((7, d), F32)], axis=0)
    mod = _ada_call(cc, w_ada, b_ada)
    modr = mod.reshape(depth * MOD_ROWS, 1, 3 * d)

    tabs = _rope_tables(s, lc, HEAD_DIM) + _rope_tables(s, lc, DF_QK)
    wp = _take_runs(w_in.astype(BF16), _in_perm(), 2)
    wo = _take_runs(w_out.astype(BF16), _out_perm(), 1)
    pmf = jnp.asarray(_hgrn_sum_matrix(False), BF16)
    pmb = jnp.asarray(_hgrn_sum_matrix(True), BF16)
    tile2 = lambda v: jnp.tile(v, 2)[None, :]
    alpha = (2.0 * depth) ** 0.25

    cx = ctx
    for l in range(depth):
        ctx_out = l < depth - 1
        za, at, sg = _inproj_call(x, cx, modr, l, wp[l], tile2(ga_q_norm_g[l]), tile2(ga_k_norm_g[l]), tabs)
        ya = _hgrn_call(za, hg_lb_logits, jnp.tile(hg_norm_g[l], 4)[None, :], pmf, pmb, l, lc)
        lam_init = 0.8 - 0.6 * math.exp(-0.3 * l)
        yb, yc, yd = _attn_bcd_call(at, lc, ctx_out, wn_sink[l], df_lambda[l], tile2(df_subln_g[l]), lam_init)
        x, cx_new = _outproj_call((ya, yb, yc, yd), sg, x, cx, modr, l, wo[l],
                                  ln_g[l][None, :], ln_b[l][None, :], ctx_out, alpha)
        cx = cx_new if ctx_out else cx
    return x
```

```python
import functools
import math

import numpy as np
import jax
import jax.numpy as jnp
from jax import lax
from jax.experimental import pallas as pl
from jax.experimental.pallas import tpu as pltpu

F32 = jnp.float32
BF16 = jnp.bfloat16

D_MODEL = 1024
GROUP_W = 256
HEAD_DIM = 64
DF_QK = 32
GRID_W = 64
WINDOW = 128
ROPE_THETA = 10000.0
LN_EPS = 1e-5
RMS_EPS = 1e-6
LOG2E = 1.0 / math.log(2.0)
HG_CHUNK = 64
HG_SUB = 8
HG_UNROLL = 6

LANES = 128
TM = 256
IN_SUB = 3
MOD_ROWS = 24

A_W = 4 * GROUP_W
ATT_W = 14 * LANES
ATT_OFF = A_W
GATE_OFF = A_W + ATT_W
IN_W = GATE_OFF + D_MODEL
B_Q, B_K, B_V = 0, 2, 3
C_Q, C_K, C_V = 4, 6, 8
D_Q, D_K, D_V = 10, 12, 13
HEAD_PERM = (0, 2, 1, 3)

NT = (((1,), (1,)), ((), ()))
TN = (((0,), (0,)), ((), ()))


def _dot(a, b, dims=None):
    if dims is None:
        return jnp.dot(a, b, preferred_element_type=F32)
    return lax.dot_general(a, b, dims, preferred_element_type=F32)


def _split2(x):
    hi = x.astype(BF16)
    lo = (x - hi.astype(F32)).astype(BF16)
    return hi, lo


def _seg_sum(x, ones_bd):
    hi, lo = _split2(x)
    return _dot(hi, ones_bd) + _dot(lo, ones_bd)


def _block_ones(n, seg):
    r = lax.broadcasted_iota(jnp.int32, (n, n), 0) // seg
    c = lax.broadcasted_iota(jnp.int32, (n, n), 1) // seg
    return jnp.where(r == c, 1.0, 0.0).astype(BF16)


def _silu(x):
    return x * jax.nn.sigmoid(x)


def _ada_kernel(c_ref, w_ref, b_ref, o_ref):
    s = _silu(c_ref[...])
    o_ref[0] = jnp.dot(s, w_ref[0], preferred_element_type=F32, precision=lax.Precision.HIGHEST) + b_ref[0]


def _ada_call(cc, w_ada, b_ada):
    depth = w_ada.shape[0]
    n3 = w_ada.shape[2]
    nb = n3 // D_MODEL
    return pl.pallas_call(
        _ada_kernel,
        grid=(depth, nb),
        in_specs=[
            pl.BlockSpec((MOD_ROWS, D_MODEL), lambda l, j: (0, 0)),
            pl.BlockSpec((1, D_MODEL, D_MODEL), lambda l, j: (l, 0, j)),
            pl.BlockSpec((1, 1, D_MODEL), lambda l, j: (l, 0, j)),
        ],
        out_specs=pl.BlockSpec((1, MOD_ROWS, D_MODEL), lambda l, j: (l, 0, j)),
        out_shape=jax.ShapeDtypeStruct((depth, MOD_ROWS, n3), F32),
        compiler_params=pltpu.CompilerParams(dimension_semantics=("arbitrary", "arbitrary"),
                                             vmem_limit_bytes=32 * 1024 * 1024),
        name="ada_mod",
    )(cc, w_ada, b_ada.reshape(depth, 1, n3))


def _rope(x, cos, sa, sb, shift):
    up = pltpu.roll(x, LANES - shift, axis=1)
    dn = pltpu.roll(x, shift, axis=1)
    return x * cos + up * sa + dn * sb


def _inproj_kernel(xa_ref, xb_ref, xc_ref, c_ref, shc_ref, scc_ref, sh_ref, sc_ref, w_ref, qn_ref, kn_ref,
                   c64_ref, sa64_ref, sb64_ref, c32_ref, sa32_ref, sb32_ref,
                   za_ref, at_ref, sg_ref, h_scr):
    t = pl.program_id(1)
    sc = 1.0 + sc_ref[0]
    sh = sh_ref[0]

    @pl.when(t == 0)
    def _():
        h_scr[0:TM, :] = (c_ref[0] * (1.0 + scc_ref[0]) + shc_ref[0]).astype(BF16)

    @pl.when(t > 0)
    def _():
        h_scr[0:TM, :] = (xa_ref[0] * sc + sh).astype(BF16)

    h_scr[TM:2 * TM, :] = (xb_ref[0] * sc + sh).astype(BF16)
    h_scr[2 * TM:3 * TM, :] = (xc_ref[0] * sc + sh).astype(BF16)
    h = h_scr[...]

    def proj(blk0, nblk):
        c0 = ATT_OFF + blk0 * LANES
        return _dot(h, w_ref[:, c0:c0 + nblk * LANES])

    def put(blk, val):
        at_ref[0, :, blk * LANES:(blk + 1) * LANES] = val.astype(BF16)

    def lanes(a, j):
        return a[:, j * LANES:(j + 1) * LANES]

    acc = _dot(h, w_ref[:, 0:A_W])
    for j in range(A_W // LANES):
        za_ref[0, j] = lanes(acc, j)

    ones64 = _block_ones(LANES, HEAD_DIM)
    c64, sa64, sb64 = c64_ref[...], sa64_ref[...], sb64_ref[...]
    c32, sa32, sb32 = c32_ref[...], sa32_ref[...], sb32_ref[...]
    q64 = HEAD_DIM // 4
    q32 = DF_QK // 4

    def rms(v, g):
        ms = _seg_sum(v * v, ones64) * (1.0 / HEAD_DIM)
        return v * lax.rsqrt(ms + RMS_EPS) * g

    acc = proj(B_Q, 4)
    for j in range(2):
        put(B_Q + j, _rope(rms(lanes(acc, j), qn_ref[...]), c64, sa64, sb64, q64) * (HEAD_DIM ** -0.5 * LOG2E))
    put(B_K, _rope(rms(lanes(acc, 2), kn_ref[...]), c64, sa64, sb64, q64))
    put(B_V, lanes(acc, 3))

    acc = proj(C_Q, 6)
    for j in range(2):
        put(C_Q + j, _rope(lanes(acc, j), c32, sa32, sb32, q32) * (DF_QK ** -0.5 * LOG2E))
        put(C_K + j, _rope(lanes(acc, 2 + j), c32, sa32, sb32, q32))
        put(C_V + j, lanes(acc, 4 + j))

    acc = proj(D_Q, 4)
    for j in range(2):
        put(D_Q + j, _rope(lanes(acc, j), c64, sa64, sb64, q64) * (HEAD_DIM ** -0.5 * LOG2E))
    put(D_K, _rope(lanes(acc, 2), c64, sa64, sb64, q64))
    put(D_V, lanes(acc, 3))

    sg_ref[0] = _silu(_dot(h, w_ref[:, GATE_OFF:IN_W])).astype(sg_ref.dtype)


def _inproj_call(x, cx, modr, layer, wp, qn, kn, tabs):
    bsz, s, _ = x.shape
    lc = cx.shape[1]
    nlt = s // TM
    t_tot = lc + s
    rows = IN_SUB * TM
    assert lc == TM and t_tot % rows == 0
    tab_spec = pl.BlockSpec((rows, LANES), lambda b, t: (t, 0))
    vec_spec = pl.BlockSpec((1, LANES), lambda b, t: (0, 0))

    def x_spec(off):
        return pl.BlockSpec((1, TM, D_MODEL), lambda b, t: (b, jnp.clip(IN_SUB * t - 1 + off, 0, nlt - 1), 0))

    def mod_spec(col, ctx_row):
        return pl.BlockSpec((1, 1, D_MODEL),
                            lambda b, t: (layer * MOD_ROWS + (MOD_ROWS - 8 if ctx_row else b), 0, col))

    return pl.pallas_call(
        _inproj_kernel,
        grid=(bsz, t_tot // rows),
        in_specs=[
            x_spec(0), x_spec(1), x_spec(2),
            pl.BlockSpec((1, TM, D_MODEL), lambda b, t: (b, 0, 0)),
            mod_spec(0, True), mod_spec(1, True), mod_spec(0, False), mod_spec(1, False),
            pl.BlockSpec((D_MODEL, IN_W), lambda b, t: (0, 0), pipeline_mode=pl.Buffered(1)),
            vec_spec, vec_spec,
            tab_spec, tab_spec, tab_spec, tab_spec, tab_spec, tab_spec,
        ],
        out_specs=[
            pl.BlockSpec((1, A_W // LANES, rows, LANES), lambda b, t: (b, 0, t, 0)),
            pl.BlockSpec((1, rows, ATT_W), lambda b, t: (b, t, 0)),
            pl.BlockSpec((1, rows, D_MODEL), lambda b, t: (b, t, 0)),
        ],
        out_shape=[
            jax.ShapeDtypeStruct((bsz, A_W // LANES, t_tot, LANES), F32),
            jax.ShapeDtypeStruct((bsz, t_tot, ATT_W), BF16),
            jax.ShapeDtypeStruct((bsz, t_tot, D_MODEL), BF16),
        ],
        scratch_shapes=[pltpu.VMEM((rows, D_MODEL), BF16)],
        compiler_params=pltpu.CompilerParams(dimension_semantics=("arbitrary", "arbitrary"),
                                             vmem_limit_bytes=56 * 1024 * 1024),
        name="in_proj",
    )(x, x, x, cx, modr, modr, modr, modr, wp, qn, kn, *tabs)


def _hgrn_levels():
    w = HG_CHUNK // 2
    out = []
    while w >= HG_SUB:
        out.append(w)
        w //= 2
    return tuple(out)


def _hgrn_time_of_row():
    nsub = HG_CHUNK // HG_SUB
    r = np.arange(HG_CHUNK)
    return (r % nsub) * HG_SUB + r // nsub


def _hgrn_sum_matrix(rev):
    c = HG_CHUNK
    t = np.arange(c)[:, None]
    j = np.arange(c)[None, :]
    mats = [j <= t, j > t]
    for w in _hgrn_levels():
        m = (t // (2 * w)) * 2 * w + w
        mats.append(np.where(t >= m, (j >= m) & (j <= t), (j > t) & (j <= m - 1)))
    if rev:
        mats = [m[::-1, ::-1] for m in mats]
    tor = _hgrn_time_of_row()
    full = np.concatenate([m[tor][:, tor] for m in mats], axis=0).astype(np.float32)
    return np.concatenate([full, full], axis=1)


def _hgrn_kernel(layer, depth, nc_ctx, za_ref, lb_ref, g_ref, pmf_ref, pmb_ref, y_ref,
                 of_scr, ob_scr, stf_scr, stb_scr):
    c = HG_CHUNK
    u = HG_SUB
    nsub = c // u
    w4 = GROUP_W
    levels = _hgrn_levels()
    t_tot = za_ref.shape[2]
    nc_tot = t_tot // c
    log2e = 1.0 / math.log(2.0)

    lbl = [lb_ref[j] for j in range(depth)]
    mx = functools.reduce(jnp.maximum, lbl)
    ex = [jnp.exp(v - mx) for v in lbl]
    den = functools.reduce(lambda a, b: a + b, ex)
    lower = jnp.zeros_like(den)
    for j in range(1, layer + 1):
        lower = lower + ex[j] / den

    r256 = lax.broadcasted_iota(jnp.int32, (w4, w4), 0)
    c256 = lax.broadcasted_iota(jnp.int32, (w4, w4), 1)
    headmask = (r256 // HEAD_DIM) == (c256 // HEAD_DIM)
    ones_bd = jnp.where(headmask, 1.0, 0.0).astype(BF16)
    rrow = lax.broadcasted_iota(jnp.int32, (c, w4), 0)
    trow = (rrow % nsub) * u + rrow // nsub
    rcol = lax.broadcasted_iota(jnp.int32, (c, w4), 1) % HEAD_DIM
    scol = (rcol % nsub) * u + rcol // nsub

    def load(r0, slab0):
        return jnp.concatenate(
            [jnp.concatenate([za_ref[0, slab0 + j, pl.ds(r0 + p, nsub, stride=u), :] for p in range(u)], axis=0)
             for j in range(2)], axis=1)

    def chunk(r0, rev, st_scr):
        zq = load(r0, 0)
        v = load(r0, 2)
        zf = load(r0, 6 if rev else 4)
        lb = lower[1:2] if rev else lower[0:1]
        q = _silu(zq)
        f = lb + (1.0 - lb) * jax.nn.sigmoid(zf)
        k = 1.0 - f
        lf2 = jnp.log(f) * log2e

        hi, lo = _split2(lf2)
        pm = pmb_ref[...] if rev else pmf_ref[...]
        esum = _dot(pm, jnp.concatenate([hi, lo], axis=0))
        e_cum = esum[0:c]
        e_after = esum[c:2 * c]
        e_tot = e_cum[0:1] if rev else e_cum[c - 1:c]

        st = st_scr[...]
        vb = v.astype(BF16)
        o = _dot((q * jnp.exp2(e_cum)).astype(BF16), st.astype(BF16), NT)

        att = jnp.zeros((c, w4), F32)
        for li, w in enumerate(levels):
            ew = jnp.exp2(esum[(2 + li) * c:(3 + li) * c])
            is_q = ((trow // w) % 2 == 0) if rev else ((trow // w) % 2 == 1)
            qw = jnp.where(is_q, q * ew, 0.0).astype(BF16)
            kw = jnp.where(is_q, 0.0, k * ew).astype(BF16)
            kbd = jnp.where(headmask, jnp.concatenate([kw] * 4, axis=0), jnp.zeros((), BF16))
            a = _dot(qw, kbd, NT)
            att = att + jnp.where((trow // (2 * w)) == (scol // (2 * w)), a, 0.0)
        vbd = jnp.where(headmask, jnp.concatenate([vb] * 4, axis=0), jnp.zeros((), BF16))
        o = o + _dot(att.astype(BF16), vbd)

        grp = lambda a, p: a[p * nsub:(p + 1) * nsub]
        pairs = [(p, s) for p in range(u) for s in range(u) if (s >= p if rev else s <= p)]
        tiles = []
        for p, s in pairs:
            qk = grp(q, p) * grp(k, s)
            tiles.append(qk if p == s else qk * jnp.exp2(grp(e_cum, p) - grp(e_cum, s)))
        red = _dot(jnp.concatenate(tiles, axis=0).astype(BF16), ones_bd)
        ods = [None] * u
        for i, (p, s) in enumerate(pairs):
            term = red[i * nsub:(i + 1) * nsub] * grp(v, s)
            ods[p] = term if ods[p] is None else ods[p] + term
        o = o + jnp.concatenate(ods, axis=0)

        upd = _dot(vb, (k * jnp.exp2(e_after)).astype(BF16), TN)
        st_scr[...] = st * jnp.exp2(e_tot) + jnp.where(headmask, upd, 0.0)
        return o

    def store(o_scr, r0, o):
        for j in range(2):
            for p in range(u):
                o_scr[j, pl.ds(r0 + p, nsub, stride=u), :] = o[p * nsub:(p + 1) * nsub, j * LANES:(j + 1) * LANES]

    stf_scr[...] = jnp.zeros_like(stf_scr)
    stb_scr[...] = jnp.zeros_like(stb_scr)

    def body(i2, carry):
        for jj in range(HG_UNROLL):
            i = i2 * HG_UNROLL + jj
            r0 = pl.multiple_of(i * c, c)
            store(of_scr, r0, chunk(r0, False, stf_scr))
            ci = jnp.where(i < nc_ctx, nc_ctx - 1 - i, nc_tot - 1 - (i - nc_ctx))
            r1 = pl.multiple_of(ci * c, c)
            store(ob_scr, r1, chunk(r1, True, stb_scr))
        return carry

    lax.fori_loop(0, nc_tot // HG_UNROLL, body, 0)

    g = g_ref[...]
    nrow = 4 * c

    def norm_body(i, carry):
        r0 = pl.multiple_of(i * nrow, nrow)
        o = jnp.concatenate([of_scr[j, pl.ds(r0, nrow), :] + ob_scr[j, pl.ds(r0, nrow), :] for j in range(2)], axis=1)
        ms = _dot((o * o).astype(BF16), ones_bd) * (1.0 / HEAD_DIM)
        y_ref[0, pl.ds(r0, nrow), :] = (o * lax.rsqrt(ms + RMS_EPS) * g).astype(y_ref.dtype)
        return carry

    lax.fori_loop(0, t_tot // nrow, norm_body, 0)


def _hgrn_call(za, lb_logits, g4, pmf, pmb, layer, lc):
    bsz, nslab, t_tot, _ = za.shape
    depth = lb_logits.shape[0]
    return pl.pallas_call(
        functools.partial(_hgrn_kernel, layer, depth, lc // HG_CHUNK),
        grid=(bsz,),
        in_specs=[
            pl.BlockSpec((1, nslab, t_tot, LANES), lambda b: (b, 0, 0, 0)),
            pl.BlockSpec((depth, 2, GROUP_W), lambda b: (0, 0, 0)),
            pl.BlockSpec((1, GROUP_W), lambda b: (0, 0)),
            pl.BlockSpec(pmf.shape, lambda b: (0, 0)),
            pl.BlockSpec(pmb.shape, lambda b: (0, 0)),
        ],
        out_specs=pl.BlockSpec((1, t_tot, GROUP_W), lambda b: (b, 0, 0)),
        out_shape=jax.ShapeDtypeStruct((bsz, t_tot, GROUP_W), BF16),
        scratch_shapes=[pltpu.VMEM((2, t_tot, LANES), F32), pltpu.VMEM((2, t_tot, LANES), F32),
                        pltpu.VMEM((GROUP_W, GROUP_W), F32), pltpu.VMEM((GROUP_W, GROUP_W), F32)],
        compiler_params=pltpu.CompilerParams(dimension_semantics=("arbitrary",),
                                             vmem_limit_bytes=48 * 1024 * 1024),
        name="hgrn2",
    )(za, lb_logits, g4, pmf, pmb)


def _lane_iota(rows):
    return lax.broadcasted_iota(jnp.int32, (rows, LANES), 1)


def _mask_q(q, lane, off, width):
    return jnp.where((lane >= off) & (lane < off + width), q, jnp.zeros((), q.dtype))


def _attn_kernel(tile0, nct, lam_init, alpha, sink_ref, q_ref, kv_ref, lam_ref, g_ref, ya_ref, sg_ref, x_ref, c_ref,
                 gt_ref, w_ref, lg_ref, lb_ref, *rest):
    n_out = 2 if tile0 < nct else 1
    out_refs = rest[:n_out]
    vxb_scr, vxc0_scr, vxc1_scr, vxd_scr, l_scr, e_scr = rest[n_out:]
    t = pl.program_id(1) + tile0
    lane = _lane_iota(TM)
    t_tot = kv_ref.shape[1]
    lc = nct * TM
    win = TM + 2 * WINDOW
    nslot = l_scr.shape[0]
    lp = lam_ref[...]
    lam = (jnp.exp(jnp.sum(lp[0:1] * lp[1:2], axis=-1, keepdims=True))
           - jnp.exp(jnp.sum(lp[2:3] * lp[3:4], axis=-1, keepdims=True)) + lam_init)
    ones64 = _block_ones(LANES, HEAD_DIM)

    def blk(ref, b, rows=slice(None)):
        return ref[0, rows, b * LANES:(b + 1) * LANES]

    @pl.when(pl.program_id(1) == 0)
    def _():
        for scr, b in ((vxb_scr, B_V), (vxc0_scr, C_V), (vxc1_scr, C_V + 1), (vxd_scr, D_V)):
            scr[:, 0:LANES] = blk(kv_ref, b)
            scr[:, LANES:2 * LANES] = jnp.ones((t_tot, LANES), BF16)

    def run(latent):
        nk = t_tot if latent else lc
        ncol_d = lc + win if latent else lc
        if latent:
            base = (t - nct) * TM
            wstart = pl.multiple_of(jnp.clip(base - WINDOW, 0, t_tot - lc - win), WINDOW)
            qpos = base + lax.broadcasted_iota(jnp.int32, (TM, win), 0)
            kpos = wstart + lax.broadcasted_iota(jnp.int32, (TM, win), 1)
            allowed = jnp.abs(kpos - qpos) <= WINDOW

        def plain_softmax(s, ncol):
            mx = jnp.max(l_scr[s, :, 0:ncol], axis=-1, keepdims=True)
            e_scr[s, :, 0:ncol] = jnp.exp2(l_scr[s, :, 0:ncol] - mx).astype(BF16)

        def ratio(acc, extra=0.0):
            return acc[:, 0:LANES] / (acc[:, LANES:2 * LANES] + extra)

        def stage_b(hd):
            j, half = divmod(hd, 2)

            def logits(s):
                qm = _mask_q(blk(q_ref, B_Q + j), lane, half * HEAD_DIM, HEAD_DIM)
                l_scr[s, :, 0:nk] = _dot(qm, blk(kv_ref, B_K, slice(0, nk)), NT)

            return logits, lambda s: plain_softmax(s, nk), lambda s, _: ratio(_dot(e_scr[s, :, 0:nk], vxb_scr[0:nk, :]))

        def stage_c(grp):
            b, off = grp // 4, (grp % 4) * DF_QK
            vx = vxc0_scr if grp < 4 else vxc1_scr

            def logits(s):
                qm = _mask_q(blk(q_ref, C_Q + b), lane, off, DF_QK)
                l_scr[s, :, 0:nk] = _dot(qm, blk(kv_ref, C_K + b, slice(0, nk)), NT)

            return logits, lambda s: plain_softmax(s, nk), lambda s, _: ratio(_dot(e_scr[s, :, 0:nk], vx[0:nk, :]))

        def stage_d(hd):
            j, half = divmod(hd, 2)

            def logits(s):
                qm = _mask_q(blk(q_ref, D_Q + j), lane, half * HEAD_DIM, HEAD_DIM)
                l_scr[s, :, 0:lc] = _dot(qm, blk(kv_ref, D_K, slice(0, lc)), NT)
                if latent:
                    lw = _dot(qm, blk(kv_ref, D_K, pl.ds(lc + wstart, win)), NT)
                    l_scr[s, :, lc:ncol_d] = jnp.where(allowed, lw, -jnp.inf)

            def softmax(s):
                sink = sink_ref[HEAD_PERM[hd]] * LOG2E
                mx = jnp.maximum(jnp.max(l_scr[s, :, 0:ncol_d], axis=-1, keepdims=True), sink)
                e_scr[s, :, 0:ncol_d] = jnp.exp2(l_scr[s, :, 0:ncol_d] - mx).astype(BF16)
                return jnp.exp2(sink - mx)

            def values(s, e_sink):
                acc = _dot(e_scr[s, :, 0:lc], vxd_scr[0:lc, :])
                if latent:
                    acc = acc + _dot(e_scr[s, :, lc:ncol_d], vxd_scr[pl.ds(lc + wstart, win), :])
                return ratio(acc, e_sink)

            return logits, softmax, values

        stages = [stage_b(i) for i in range(4)] + [stage_c(i) for i in range(8)] + [stage_d(i) for i in range(4)]
        res = []
        stages[0][0](0)
        for i, (_, softmax, values) in enumerate(stages):
            if i + 1 < len(stages):
                stages[i + 1][0]((i + 1) % nslot)
            res.append(values(i % nslot, softmax(i % nslot)))
        rb, rc, rd = res[0:4], res[4:12], res[12:16]

        def pair(a, b):
            return jnp.where(lane < HEAD_DIM, a, b)

        yb, yc, yd = [], [], []
        for j in range(2):
            yb.append(pair(rb[2 * j], rb[2 * j + 1]).astype(BF16))
            yd.append(pair(rd[2 * j], rd[2 * j + 1]).astype(BF16))
            hc = [rc[2 * hd] - lam * rc[2 * hd + 1] for hd in (2 * j, 2 * j + 1)]
            o = pair(hc[0], hc[1])
            ms = _seg_sum(o * o, ones64) * (1.0 / HEAD_DIM)
            yc.append((o * lax.rsqrt(ms + RMS_EPS) * g_ref[...] * (1.0 - lam_init)).astype(BF16))

        y = jnp.concatenate([ya_ref[0]] + yb + yc + yd, axis=-1) * sg_ref[0]
        z = alpha * (x_ref[0] if latent else c_ref[0]) + gt_ref[0] * _dot(y, w_ref[...])
        zc = z - jnp.mean(z, axis=-1, keepdims=True)
        var = jnp.mean(zc * zc, axis=-1, keepdims=True)
        out_refs[0 if latent else 1][0] = zc * lax.rsqrt(var + LN_EPS) * lg_ref[...] + lb_ref[...]

    if tile0 < nct:
        @pl.when(t < nct)
        def _():
            run(False)

    @pl.when(t >= nct)
    def _():
        run(True)


def _attn_out_call(at, ya, sg, x, cx, modr, layer, wo, lng, lnb, ctx_out, alpha, sink, lam, g2, lam_init):
    bsz, t_tot, _ = at.shape
    s, lc = x.shape[1], cx.shape[1]
    nct = lc // TM
    ntile = t_tot // TM
    tile0 = 0 if ctx_out else nct
    vx = pltpu.VMEM((t_tot, 2 * LANES), BF16)
    vec = pl.BlockSpec((1, D_MODEL), lambda b, t: (0, 0))
    out_specs = [pl.BlockSpec((1, TM, D_MODEL), lambda b, t: (b, jnp.maximum(t + tile0 - nct, 0), 0))]
    out_shape = [jax.ShapeDtypeStruct((bsz, s, D_MODEL), F32)]
    if ctx_out:
        out_specs.append(pl.BlockSpec((1, TM, D_MODEL), lambda b, t: (b, jnp.minimum(t, nct - 1), 0)))
        out_shape.append(jax.ShapeDtypeStruct((bsz, lc, D_MODEL), F32))
    res = pl.pallas_call(
        functools.partial(_attn_kernel, tile0, nct, lam_init, alpha),
        grid=(bsz, ntile - tile0),
        in_specs=[
            pl.BlockSpec(memory_space=pltpu.SMEM),
            pl.BlockSpec((1, TM, ATT_W), lambda b, t: (b, t + tile0, 0)),
            pl.BlockSpec((1, t_tot, ATT_W), lambda b, t: (b, 0, 0)),
            pl.BlockSpec(lam.shape, lambda b, t: (0, 0)),
            pl.BlockSpec((1, LANES), lambda b, t: (0, 0)),
            pl.BlockSpec((1, TM, GROUP_W), lambda b, t: (b, t + tile0, 0)),
            pl.BlockSpec((1, TM, D_MODEL), lambda b, t: (b, t + tile0, 0)),
            pl.BlockSpec((1, TM, D_MODEL), lambda b, t: (b, jnp.maximum(t + tile0 - nct, 0), 0)),
            pl.BlockSpec((1, TM, D_MODEL), lambda b, t: (b, jnp.minimum(t + tile0, nct - 1), 0)),
            pl.BlockSpec((1, 1, D_MODEL),
                         lambda b, t: (layer * MOD_ROWS + jnp.where(t + tile0 < nct, MOD_ROWS - 8, b), 0, 2)),
            pl.BlockSpec((D_MODEL, D_MODEL), lambda b, t: (0, 0), pipeline_mode=pl.Buffered(1)),
            vec, vec,
        ],
        out_specs=out_specs,
        out_shape=out_shape,
        scratch_shapes=[vx, vx, vx, vx, pltpu.VMEM((4, TM, t_tot), F32), pltpu.VMEM((4, TM, t_tot), BF16)],
        compiler_params=pltpu.CompilerParams(dimension_semantics=("arbitrary", "arbitrary"),
                                             vmem_limit_bytes=56 * 1024 * 1024),
        name="attn_out",
    )(sink, at, at, lam, g2, ya, sg, x, cx, modr, wo, lng, lnb)
    return (res[0], res[1]) if ctx_out else (res[0], None)


def _in_perm():
    hp = np.array(HEAD_PERM)
    head_cols = (hp[:, None] * HEAD_DIM + np.arange(HEAD_DIM)[None, :]).reshape(-1)
    idx = np.arange(4 * GROUP_W + 2 * GROUP_W + 3 * GROUP_W + 2 * GROUP_W + D_MODEL)
    b0 = A_W
    d0 = A_W + 2 * GROUP_W + 3 * GROUP_W
    idx[b0:b0 + GROUP_W] = b0 + head_cols
    idx[d0:d0 + GROUP_W] = d0 + head_cols
    idx[GATE_OFF + GROUP_W:GATE_OFF + 2 * GROUP_W] = GATE_OFF + GROUP_W + head_cols
    idx[GATE_OFF + 3 * GROUP_W:GATE_OFF + 4 * GROUP_W] = GATE_OFF + 3 * GROUP_W + head_cols
    return idx


def _out_perm():
    hp = np.array(HEAD_PERM)
    head_rows = (hp[:, None] * HEAD_DIM + np.arange(HEAD_DIM)[None, :]).reshape(-1)
    idx = np.arange(D_MODEL)
    idx[GROUP_W:2 * GROUP_W] = GROUP_W + head_rows
    idx[3 * GROUP_W:4 * GROUP_W] = 3 * GROUP_W + head_rows
    return idx


def _take_runs(a, idx, axis):
    idx = [int(v) for v in idx]
    runs, start = [], idx[0]
    for prev, cur in zip(idx, idx[1:]):
        if cur != prev + 1:
            runs.append((start, prev + 1))
            start = cur
    runs.append((start, idx[-1] + 1))
    return jnp.concatenate([lax.slice_in_dim(a, lo, hi, axis=axis) for lo, hi in runs], axis=axis)


def _rope_tables(n_lat, n_ctx, dim):
    rows = n_lat // GRID_W
    row = jnp.repeat(jnp.arange(rows, dtype=F32), GRID_W)
    col = jnp.broadcast_to(jnp.arange(GRID_W, dtype=F32), (rows, GRID_W)).reshape(-1)
    d_axis = dim // 2
    inv = ROPE_THETA ** (-jnp.arange(0, d_axis, 2, dtype=F32) / d_axis)
    ang_r = row[:, None] * inv
    ang_c = col[:, None] * inv
    ang = jnp.concatenate([ang_r, ang_r, ang_c, ang_c], axis=-1)
    cos, sin = jnp.cos(ang), jnp.sin(ang)
    first = (np.arange(dim) % (dim // 2)) < (dim // 4)
    sa = jnp.where(first[None, :], -sin, 0.0)
    sb = jnp.where(first[None, :], 0.0, sin)
    reps = LANES // dim
    pad = lambda a, fill: jnp.concatenate([jnp.full((n_ctx, LANES), fill, F32), jnp.tile(a, (1, reps))], axis=0)
    return pad(cos, 1.0), pad(sa, 0.0), pad(sb, 0.0)


def kernel(x, c, ctx, c_ctx, w_in, w_out, w_ada, b_ada, ln_g, ln_b, hg_lb_logits, hg_norm_g, ga_q_norm_g,
           ga_k_norm_g, df_lambda, df_subln_g, wn_sink):
    bsz, s, d = x.shape
    lc = ctx.shape[1]
    depth = w_in.shape[0]
    assert d == D_MODEL and s % TM == 0 and lc % TM == 0 and s >= TM + 2 * WINDOW and bsz <= MOD_ROWS - 8
    assert w_in.shape[2] == IN_W

    cc = jnp.concatenate([c, jnp.zeros((MOD_ROWS - 8 - bsz, d), F32), c_ctx[None, :],
                          jnp.zeros((7, d), F32)], axis=0)
    mod = _ada_call(cc, w_ada, b_ada)
    modr = mod.reshape(depth * MOD_ROWS, 1, 3 * d)

    tabs = _rope_tables(s, lc, HEAD_DIM) + _rope_tables(s, lc, DF_QK)
    wp = _take_runs(w_in.astype(BF16), _in_perm(), 2)
    wo = _take_runs(w_out.astype(BF16), _out_perm(), 1)
    pmf = jnp.asarray(_hgrn_sum_matrix(False), BF16)
    pmb = jnp.asarray(_hgrn_sum_matrix(True), BF16)
    tile2 = lambda v: jnp.tile(v, 2)[None, :]
    alpha = (2.0 * depth) ** 0.25

    cx = ctx
    for l in range(depth):
        ctx_out = l < depth - 1
        za, at, sg = _inproj_call(x, cx, modr, l, wp[l], tile2(ga_q_norm_g[l]), tile2(ga_k_norm_g[l]), tabs)
        ya = _hgrn_call(za, hg_lb_logits, jnp.tile(hg_norm_g[l], 4)[None, :], pmf, pmb, l, lc)
        lam_init = 0.8 - 0.6 * math.exp(-0.3 * l)
        x, cx_new = _attn_out_call(at, ya, sg, x, cx, modr, l, wo[l], ln_g[l][None, :], ln_b[l][None, :], ctx_out,
                                   alpha, wn_sink[l], df_lambda[l], tile2(df_subln_g[l]), lam_init)
        cx = cx_new if ctx_out else cx
    return x
```

```python
import functools
import math

import numpy as np
import jax
import jax.numpy as jnp
from jax import lax
from jax.experimental import pallas as pl
from jax.experimental.pallas import tpu as pltpu

F32 = jnp.float32
BF16 = jnp.bfloat16

D_MODEL = 1024
GROUP_W = 256
HEAD_DIM = 64
DF_QK = 32
GRID_W = 64
WINDOW = 128
ROPE_THETA = 10000.0
LN_EPS = 1e-5
RMS_EPS = 1e-6
LOG2E = 1.0 / math.log(2.0)
HG_CHUNK = 64
HG_SUB = 8
HG_UNROLL = 6

LANES = 128
TM = 256
IN_SUB = 3
MOD_ROWS = 24

A_W = 4 * GROUP_W
ATT_W = 14 * LANES
ATT_OFF = A_W
GATE_OFF = A_W + ATT_W
IN_W = GATE_OFF + D_MODEL
B_Q, B_K, B_V = 0, 2, 3
C_Q, C_K, C_V = 4, 6, 8
D_Q, D_K, D_V = 10, 12, 13
HEAD_PERM = (0, 2, 1, 3)

NT = (((1,), (1,)), ((), ()))
TN = (((0,), (0,)), ((), ()))


def _dot(a, b, dims=None):
    if dims is None:
        return jnp.dot(a, b, preferred_element_type=F32)
    return lax.dot_general(a, b, dims, preferred_element_type=F32)


def _split2(x):
    hi = x.astype(BF16)
    lo = (x - hi.astype(F32)).astype(BF16)
    return hi, lo


def _seg_sum(x, ones_bd):
    hi, lo = _split2(x)
    return _dot(hi, ones_bd) + _dot(lo, ones_bd)


def _block_ones(n, seg):
    r = lax.broadcasted_iota(jnp.int32, (n, n), 0) // seg
    c = lax.broadcasted_iota(jnp.int32, (n, n), 1) // seg
    return jnp.where(r == c, 1.0, 0.0).astype(BF16)


def _silu(x):
    return x * jax.nn.sigmoid(x)


def _ada_kernel(c_ref, w_ref, b_ref, o_ref):
    s = _silu(c_ref[...])
    o_ref[0] = jnp.dot(s, w_ref[0], preferred_element_type=F32, precision=lax.Precision.HIGHEST) + b_ref[0]


def _ada_call(cc, w_ada, b_ada):
    depth = w_ada.shape[0]
    n3 = w_ada.shape[2]
    nb = n3 // D_MODEL
    return pl.pallas_call(
        _ada_kernel,
        grid=(depth, nb),
        in_specs=[
            pl.BlockSpec((MOD_ROWS, D_MODEL), lambda l, j: (0, 0)),
            pl.BlockSpec((1, D_MODEL, D_MODEL), lambda l, j: (l, 0, j)),
            pl.BlockSpec((1, 1, D_MODEL), lambda l, j: (l, 0, j)),
        ],
        out_specs=pl.BlockSpec((1, MOD_ROWS, D_MODEL), lambda l, j: (l, 0, j)),
        out_shape=jax.ShapeDtypeStruct((depth, MOD_ROWS, n3), F32),
        compiler_params=pltpu.CompilerParams(dimension_semantics=("arbitrary", "arbitrary"),
                                             vmem_limit_bytes=32 * 1024 * 1024),
        name="ada_mod",
    )(cc, w_ada, b_ada.reshape(depth, 1, n3))


def _rope(x, cos, sa, sb, shift):
    up = pltpu.roll(x, LANES - shift, axis=1)
    dn = pltpu.roll(x, shift, axis=1)
    return x * cos + up * sa + dn * sb


def _inproj_kernel(xa_ref, xb_ref, xc_ref, c_ref, shc_ref, scc_ref, sh_ref, sc_ref, w_ref, qn_ref, kn_ref,
                   c64_ref, sa64_ref, sb64_ref, c32_ref, sa32_ref, sb32_ref,
                   za_ref, at_ref, sg_ref, h_scr):
    t = pl.program_id(1)
    sc = 1.0 + sc_ref[0]
    sh = sh_ref[0]

    @pl.when(t == 0)
    def _():
        h_scr[0:TM, :] = (c_ref[0] * (1.0 + scc_ref[0]) + shc_ref[0]).astype(BF16)

    @pl.when(t > 0)
    def _():
        h_scr[0:TM, :] = (xa_ref[0] * sc + sh).astype(BF16)

    h_scr[TM:2 * TM, :] = (xb_ref[0] * sc + sh).astype(BF16)
    h_scr[2 * TM:3 * TM, :] = (xc_ref[0] * sc + sh).astype(BF16)
    h = h_scr[...]

    def proj(blk0, nblk):
        c0 = ATT_OFF + blk0 * LANES
        return _dot(h, w_ref[:, c0:c0 + nblk * LANES])

    def put(blk, val):
        at_ref[0, :, blk * LANES:(blk + 1) * LANES] = val.astype(BF16)

    def lanes(a, j):
        return a[:, j * LANES:(j + 1) * LANES]

    acc = _dot(h, w_ref[:, 0:A_W])
    for j in range(A_W // LANES):
        za_ref[0, j] = lanes(acc, j)

    ones64 = _block_ones(LANES, HEAD_DIM)
    c64, sa64, sb64 = c64_ref[...], sa64_ref[...], sb64_ref[...]
    c32, sa32, sb32 = c32_ref[...], sa32_ref[...], sb32_ref[...]
    q64 = HEAD_DIM // 4
    q32 = DF_QK // 4

    def rms(v, g):
        ms = _seg_sum(v * v, ones64) * (1.0 / HEAD_DIM)
        return v * lax.rsqrt(ms + RMS_EPS) * g

    acc = proj(B_Q, 4)
    for j in range(2):
        put(B_Q + j, _rope(rms(lanes(acc, j), qn_ref[...]), c64, sa64, sb64, q64) * (HEAD_DIM ** -0.5 * LOG2E))
    put(B_K, _rope(rms(lanes(acc, 2), kn_ref[...]), c64, sa64, sb64, q64))
    put(B_V, lanes(acc, 3))

    acc = proj(C_Q, 6)
    for j in range(2):
        put(C_Q + j, _rope(lanes(acc, j), c32, sa32, sb32, q32) * (DF_QK ** -0.5 * LOG2E))
        put(C_K + j, _rope(lanes(acc, 2 + j), c32, sa32, sb32, q32))
        put(C_V + j, lanes(acc, 4 + j))

    acc = proj(D_Q, 4)
    for j in range(2):
        put(D_Q + j, _rope(lanes(acc, j), c64, sa64, sb64, q64) * (HEAD_DIM ** -0.5 * LOG2E))
    put(D_K, _rope(lanes(acc, 2), c64, sa64, sb64, q64))
    put(D_V, lanes(acc, 3))

    sg_ref[0] = _silu(_dot(h, w_ref[:, GATE_OFF:IN_W])).astype(sg_ref.dtype)


def _inproj_call(x, cx, modr, layer, wp, qn, kn, tabs):
    bsz, s, _ = x.shape
    lc = cx.shape[1]
    nlt = s // TM
    t_tot = lc + s
    rows = IN_SUB * TM
    assert lc == TM and t_tot % rows == 0
    tab_spec = pl.BlockSpec((rows, LANES), lambda b, t: (t, 0))
    vec_spec = pl.BlockSpec((1, LANES), lambda b, t: (0, 0))

    def x_spec(off):
        return pl.BlockSpec((1, TM, D_MODEL), lambda b, t: (b, jnp.clip(IN_SUB * t - 1 + off, 0, nlt - 1), 0))

    def mod_spec(col, ctx_row):
        return pl.BlockSpec((1, 1, D_MODEL),
                            lambda b, t: (layer * MOD_ROWS + (MOD_ROWS - 8 if ctx_row else b), 0, col))

    return pl.pallas_call(
        _inproj_kernel,
        grid=(bsz, t_tot // rows),
        in_specs=[
            x_spec(0), x_spec(1), x_spec(2),
            pl.BlockSpec((1, TM, D_MODEL), lambda b, t: (b, 0, 0)),
            mod_spec(0, True), mod_spec(1, True), mod_spec(0, False), mod_spec(1, False),
            pl.BlockSpec((D_MODEL, IN_W), lambda b, t: (0, 0), pipeline_mode=pl.Buffered(1)),
            vec_spec, vec_spec,
            tab_spec, tab_spec, tab_spec, tab_spec, tab_spec, tab_spec,
        ],
        out_specs=[
            pl.BlockSpec((1, A_W // LANES, rows, LANES), lambda b, t: (b, 0, t, 0)),
            pl.BlockSpec((1, rows, ATT_W), lambda b, t: (b, t, 0)),
            pl.BlockSpec((1, rows, D_MODEL), lambda b, t: (b, t, 0)),
        ],
        out_shape=[
            jax.ShapeDtypeStruct((bsz, A_W // LANES, t_tot, LANES), F32),
            jax.ShapeDtypeStruct((bsz, t_tot, ATT_W), BF16),
            jax.ShapeDtypeStruct((bsz, t_tot, D_MODEL), BF16),
        ],
        scratch_shapes=[pltpu.VMEM((rows, D_MODEL), BF16)],
        compiler_params=pltpu.CompilerParams(dimension_semantics=("arbitrary", "arbitrary"),
                                             vmem_limit_bytes=56 * 1024 * 1024),
        name="in_proj",
    )(x, x, x, cx, modr, modr, modr, modr, wp, qn, kn, *tabs)


def _hgrn_levels():
    w = HG_CHUNK // 2
    out = []
    while w >= HG_SUB:
        out.append(w)
        w //= 2
    return tuple(out)


def _hgrn_time_of_row():
    nsub = HG_CHUNK // HG_SUB
    r = np.arange(HG_CHUNK)
    return (r % nsub) * HG_SUB + r // nsub


def _hgrn_sum_matrix(rev):
    c = HG_CHUNK
    t = np.arange(c)[:, None]
    j = np.arange(c)[None, :]
    mats = [j <= t, j > t]
    for w in _hgrn_levels():
        m = (t // (2 * w)) * 2 * w + w
        mats.append(np.where(t >= m, (j >= m) & (j <= t), (j > t) & (j <= m - 1)))
    if rev:
        mats = [m[::-1, ::-1] for m in mats]
    tor = _hgrn_time_of_row()
    full = np.concatenate([m[tor][:, tor] for m in mats], axis=0).astype(np.float32)
    return np.concatenate([full, full], axis=1)


def _hgrn_kernel(layer, depth, nc_ctx, za_ref, lb_ref, g_ref, pmf_ref, pmb_ref, y_ref,
                 of_scr, ob_scr, stf_scr, stb_scr):
    c = HG_CHUNK
    u = HG_SUB
    nsub = c // u
    w4 = GROUP_W
    levels = _hgrn_levels()
    t_tot = za_ref.shape[2]
    nc_tot = t_tot // c
    log2e = 1.0 / math.log(2.0)

    lbl = [lb_ref[j] for j in range(depth)]
    mx = functools.reduce(jnp.maximum, lbl)
    ex = [jnp.exp(v - mx) for v in lbl]
    den = functools.reduce(lambda a, b: a + b, ex)
    lower = jnp.zeros_like(den)
    for j in range(1, layer + 1):
        lower = lower + ex[j] / den

    r256 = lax.broadcasted_iota(jnp.int32, (w4, w4), 0)
    c256 = lax.broadcasted_iota(jnp.int32, (w4, w4), 1)
    headmask = (r256 // HEAD_DIM) == (c256 // HEAD_DIM)
    ones_bd = jnp.where(headmask, 1.0, 0.0).astype(BF16)
    rrow = lax.broadcasted_iota(jnp.int32, (c, w4), 0)
    trow = (rrow % nsub) * u + rrow // nsub
    rcol = lax.broadcasted_iota(jnp.int32, (c, w4), 1) % HEAD_DIM
    scol = (rcol % nsub) * u + rcol // nsub

    def load(r0, slab0):
        return jnp.concatenate(
            [jnp.concatenate([za_ref[0, slab0 + j, pl.ds(r0 + p, nsub, stride=u), :] for p in range(u)], axis=0)
             for j in range(2)], axis=1)

    def chunk(r0, rev, st_scr):
        zq = load(r0, 0)
        v = load(r0, 2)
        zf = load(r0, 6 if rev else 4)
        lb = lower[1:2] if rev else lower[0:1]
        q = _silu(zq)
        f = lb + (1.0 - lb) * jax.nn.sigmoid(zf)
        k = 1.0 - f
        lf2 = jnp.log(f) * log2e

        hi, lo = _split2(lf2)
        pm = pmb_ref[...] if rev else pmf_ref[...]
        esum = _dot(pm, jnp.concatenate([hi, lo], axis=0))
        e_cum = esum[0:c]
        e_after = esum[c:2 * c]
        e_tot = e_cum[0:1] if rev else e_cum[c - 1:c]

        st = st_scr[...]
        vb = v.astype(BF16)
        o = _dot((q * jnp.exp2(e_cum)).astype(BF16), st.astype(BF16), NT)

        att = jnp.zeros((c, w4), F32)
        for li, w in enumerate(levels):
            ew = jnp.exp2(esum[(2 + li) * c:(3 + li) * c])
            is_q = ((trow // w) % 2 == 0) if rev else ((trow // w) % 2 == 1)
            qw = jnp.where(is_q, q * ew, 0.0).astype(BF16)
            kw = jnp.where(is_q, 0.0, k * ew).astype(BF16)
            kbd = jnp.where(headmask, jnp.concatenate([kw] * 4, axis=0), jnp.zeros((), BF16))
            a = _dot(qw, kbd, NT)
            att = att + jnp.where((trow // (2 * w)) == (scol // (2 * w)), a, 0.0)
        vbd = jnp.where(headmask, jnp.concatenate([vb] * 4, axis=0), jnp.zeros((), BF16))
        o = o + _dot(att.astype(BF16), vbd)

        grp = lambda a, p: a[p * nsub:(p + 1) * nsub]
        pairs = [(p, s) for p in range(u) for s in range(u) if (s >= p if rev else s <= p)]
        tiles = []
        for p, s in pairs:
            qk = grp(q, p) * grp(k, s)
            tiles.append(qk if p == s else qk * jnp.exp2(grp(e_cum, p) - grp(e_cum, s)))
        red = _dot(jnp.concatenate(tiles, axis=0).astype(BF16), ones_bd)
        ods = [None] * u
        for i, (p, s) in enumerate(pairs):
            term = red[i * nsub:(i + 1) * nsub] * grp(v, s)
            ods[p] = term if ods[p] is None else ods[p] + term
        o = o + jnp.concatenate(ods, axis=0)

        upd = _dot(vb, (k * jnp.exp2(e_after)).astype(BF16), TN)
        st_scr[...] = st * jnp.exp2(e_tot) + jnp.where(headmask, upd, 0.0)
        return o

    def store(o_scr, r0, o):
        for j in range(2):
            for p in range(u):
                o_scr[j, pl.ds(r0 + p, nsub, stride=u), :] = o[p * nsub:(p + 1) * nsub, j * LANES:(j + 1) * LANES]

    stf_scr[...] = jnp.zeros_like(stf_scr)
    stb_scr[...] = jnp.zeros_like(stb_scr)

    def body(i2, carry):
        for jj in range(HG_UNROLL):
            i = i2 * HG_UNROLL + jj
            r0 = pl.multiple_of(i * c, c)
            store(of_scr, r0, chunk(r0, False, stf_scr))
            ci = jnp.where(i < nc_ctx, nc_ctx - 1 - i, nc_tot - 1 - (i - nc_ctx))
            r1 = pl.multiple_of(ci * c, c)
            store(ob_scr, r1, chunk(r1, True, stb_scr))
        return carry

    lax.fori_loop(0, nc_tot // HG_UNROLL, body, 0)

    g = g_ref[...]
    nrow = 4 * c

    def norm_body(i, carry):
        r0 = pl.multiple_of(i * nrow, nrow)
        o = jnp.concatenate([of_scr[j, pl.ds(r0, nrow), :] + ob_scr[j, pl.ds(r0, nrow), :] for j in range(2)], axis=1)
        ms = _dot((o * o).astype(BF16), ones_bd) * (1.0 / HEAD_DIM)
        y_ref[0, pl.ds(r0, nrow), :] = (o * lax.rsqrt(ms + RMS_EPS) * g).astype(y_ref.dtype)
        return carry

    lax.fori_loop(0, t_tot // nrow, norm_body, 0)


def _hgrn_call(za, lb_logits, g4, pmf, pmb, layer, lc):
    bsz, nslab, t_tot, _ = za.shape
    depth = lb_logits.shape[0]
    return pl.pallas_call(
        functools.partial(_hgrn_kernel, layer, depth, lc // HG_CHUNK),
        grid=(bsz,),
        in_specs=[
            pl.BlockSpec((1, nslab, t_tot, LANES), lambda b: (b, 0, 0, 0)),
            pl.BlockSpec((depth, 2, GROUP_W), lambda b: (0, 0, 0)),
            pl.BlockSpec((1, GROUP_W), lambda b: (0, 0)),
            pl.BlockSpec(pmf.shape, lambda b: (0, 0)),
            pl.BlockSpec(pmb.shape, lambda b: (0, 0)),
        ],
        out_specs=pl.BlockSpec((1, t_tot, GROUP_W), lambda b: (b, 0, 0)),
        out_shape=jax.ShapeDtypeStruct((bsz, t_tot, GROUP_W), BF16),
        scratch_shapes=[pltpu.VMEM((2, t_tot, LANES), F32), pltpu.VMEM((2, t_tot, LANES), F32),
                        pltpu.VMEM((GROUP_W, GROUP_W), F32), pltpu.VMEM((GROUP_W, GROUP_W), F32)],
        compiler_params=pltpu.CompilerParams(dimension_semantics=("arbitrary",),
                                             vmem_limit_bytes=48 * 1024 * 1024),
        name="hgrn2",
    )(za, lb_logits, g4, pmf, pmb)


def _lane_iota(rows):
    return lax.broadcasted_iota(jnp.int32, (rows, LANES), 1)


def _mask_q(q, lane, off, width):
    return jnp.where((lane >= off) & (lane < off + width), q, jnp.zeros((), q.dtype))


def _attn_kernel(tile0, nct, lam_init, alpha, sink_ref, q_ref, kv_ref, lam_ref, g_ref, ya_ref, sg_ref, x_ref, c_ref,
                 gt_ref, w_ref, lg_ref, lb_ref, *rest):
    n_out = 2 if tile0 < nct else 1
    out_refs = rest[:n_out]
    vxb_scr, vxc0_scr, vxc1_scr, vxd_scr, l_scr, e_scr = rest[n_out:]
    t = pl.program_id(1) + tile0
    lane = _lane_iota(TM)
    t_tot = kv_ref.shape[1]
    lc = nct * TM
    win = TM + 2 * WINDOW
    nslot = l_scr.shape[0]
    lp = lam_ref[...]
    lam = (jnp.exp(jnp.sum(lp[0:1] * lp[1:2], axis=-1, keepdims=True))
           - jnp.exp(jnp.sum(lp[2:3] * lp[3:4], axis=-1, keepdims=True)) + lam_init)
    ones64 = _block_ones(LANES, HEAD_DIM)

    def blk(ref, b, rows=slice(None)):
        return ref[0, rows, b * LANES:(b + 1) * LANES]

    @pl.when(pl.program_id(1) == 0)
    def _():
        for scr, b in ((vxb_scr, B_V), (vxc0_scr, C_V), (vxc1_scr, C_V + 1), (vxd_scr, D_V)):
            scr[:, 0:LANES] = blk(kv_ref, b)
            scr[:, LANES:2 * LANES] = jnp.ones((t_tot, LANES), BF16)

    def run(latent):
        nk = t_tot if latent else lc
        ncol_d = lc + win if latent else lc
        if latent:
            base = (t - nct) * TM
            wstart = pl.multiple_of(jnp.clip(base - WINDOW, 0, t_tot - lc - win), WINDOW)
            qpos = base + lax.broadcasted_iota(jnp.int32, (TM, win), 0)
            kpos = wstart + lax.broadcasted_iota(jnp.int32, (TM, win), 1)
            allowed = jnp.abs(kpos - qpos) <= WINDOW

        def plain_softmax(s, ncol):
            mx = jnp.max(l_scr[s, :, 0:ncol], axis=-1, keepdims=True)
            e_scr[s, :, 0:ncol] = jnp.exp2(l_scr[s, :, 0:ncol] - mx).astype(BF16)

        def ratio(acc, extra=0.0):
            return acc[:, 0:LANES] / (acc[:, LANES:2 * LANES] + extra)

        def stage_b(hd):
            j, half = divmod(hd, 2)

            def logits(s):
                qm = _mask_q(blk(q_ref, B_Q + j), lane, half * HEAD_DIM, HEAD_DIM)
                l_scr[s, :, 0:nk] = _dot(qm, blk(kv_ref, B_K, slice(0, nk)), NT)

            return logits, lambda s: plain_softmax(s, nk), lambda s, _: ratio(_dot(e_scr[s, :, 0:nk], vxb_scr[0:nk, :]))

        def stage_c(grp):
            b, off = grp // 4, (grp % 4) * DF_QK
            vx = vxc0_scr if grp < 4 else vxc1_scr

            def logits(s):
                qm = _mask_q(blk(q_ref, C_Q + b), lane, off, DF_QK)
                l_scr[s, :, 0:nk] = _dot(qm, blk(kv_ref, C_K + b, slice(0, nk)), NT)

            return logits, lambda s: plain_softmax(s, nk), lambda s, _: ratio(_dot(e_scr[s, :, 0:nk], vx[0:nk, :]))

        def stage_d(hd):
            j, half = divmod(hd, 2)

            def logits(s):
                qm = _mask_q(blk(q_ref, D_Q + j), lane, half * HEAD_DIM, HEAD_DIM)
                l_scr[s, :, 0:lc] = _dot(qm, blk(kv_ref, D_K, slice(0, lc)), NT)
                if latent:
                    lw = _dot(qm, blk(kv_ref, D_K, pl.ds(lc + wstart, win)), NT)
                    l_scr[s, :, lc:ncol_d] = jnp.where(allowed, lw, -jnp.inf)

            def softmax(s):
                sink = sink_ref[HEAD_PERM[hd]] * LOG2E
                mx = jnp.maximum(jnp.max(l_scr[s, :, 0:ncol_d], axis=-1, keepdims=True), sink)
                e_scr[s, :, 0:ncol_d] = jnp.exp2(l_scr[s, :, 0:ncol_d] - mx).astype(BF16)
                return jnp.exp2(sink - mx)

            def values(s, e_sink):
                acc = _dot(e_scr[s, :, 0:lc], vxd_scr[0:lc, :])
                if latent:
                    acc = acc + _dot(e_scr[s, :, lc:ncol_d], vxd_scr[pl.ds(lc + wstart, win), :])
                return ratio(acc, e_sink)

            return logits, softmax, values

        stages = [stage_b(i) for i in range(4)] + [stage_c(i) for i in range(8)] + [stage_d(i) for i in range(4)]
        res = []
        ahead = 2
        for i in range(ahead):
            stages[i][0](i % nslot)
        for i, (_, softmax, values) in enumerate(stages):
            if i + ahead < len(stages):
                stages[i + ahead][0]((i + ahead) % nslot)
            res.append(values(i % nslot, softmax(i % nslot)))
        rb, rc, rd = res[0:4], res[4:12], res[12:16]

        def pair(a, b):
            return jnp.where(lane < HEAD_DIM, a, b)

        yb, yc, yd = [], [], []
        for j in range(2):
            yb.append(pair(rb[2 * j], rb[2 * j + 1]).astype(BF16))
            yd.append(pair(rd[2 * j], rd[2 * j + 1]).astype(BF16))
            hc = [rc[2 * hd] - lam * rc[2 * hd + 1] for hd in (2 * j, 2 * j + 1)]
            o = pair(hc[0], hc[1])
            ms = _seg_sum(o * o, ones64) * (1.0 / HEAD_DIM)
            yc.append((o * lax.rsqrt(ms + RMS_EPS) * g_ref[...] * (1.0 - lam_init)).astype(BF16))

        y = jnp.concatenate([ya_ref[0]] + yb + yc + yd, axis=-1) * sg_ref[0]
        z = alpha * (x_ref[0] if latent else c_ref[0]) + gt_ref[0] * _dot(y, w_ref[...])
        zc = z - jnp.mean(z, axis=-1, keepdims=True)
        var = jnp.mean(zc * zc, axis=-1, keepdims=True)
        out_refs[0 if latent else 1][0] = zc * lax.rsqrt(var + LN_EPS) * lg_ref[...] + lb_ref[...]

    if tile0 < nct:
        @pl.when(t < nct)
        def _():
            run(False)

    @pl.when(t >= nct)
    def _():
        run(True)


def _attn_out_call(at, ya, sg, x, cx, modr, layer, wo, lng, lnb, ctx_out, alpha, sink, lam, g2, lam_init):
    bsz, t_tot, _ = at.shape
    s, lc = x.shape[1], cx.shape[1]
    nct = lc // TM
    ntile = t_tot // TM
    tile0 = 0 if ctx_out else nct
    vx = pltpu.VMEM((t_tot, 2 * LANES), BF16)
    vec = pl.BlockSpec((1, D_MODEL), lambda b, t: (0, 0))
    out_specs = [pl.BlockSpec((1, TM, D_MODEL), lambda b, t: (b, jnp.maximum(t + tile0 - nct, 0), 0))]
    out_shape = [jax.ShapeDtypeStruct((bsz, s, D_MODEL), F32)]
    if ctx_out:
        out_specs.append(pl.BlockSpec((1, TM, D_MODEL), lambda b, t: (b, jnp.minimum(t, nct - 1), 0)))
        out_shape.append(jax.ShapeDtypeStruct((bsz, lc, D_MODEL), F32))
    res = pl.pallas_call(
        functools.partial(_attn_kernel, tile0, nct, lam_init, alpha),
        grid=(bsz, ntile - tile0),
        in_specs=[
            pl.BlockSpec(memory_space=pltpu.SMEM),
            pl.BlockSpec((1, TM, ATT_W), lambda b, t: (b, t + tile0, 0)),
            pl.BlockSpec((1, t_tot, ATT_W), lambda b, t: (b, 0, 0)),
            pl.BlockSpec(lam.shape, lambda b, t: (0, 0)),
            pl.BlockSpec((1, LANES), lambda b, t: (0, 0)),
            pl.BlockSpec((1, TM, GROUP_W), lambda b, t: (b, t + tile0, 0)),
            pl.BlockSpec((1, TM, D_MODEL), lambda b, t: (b, t + tile0, 0)),
            pl.BlockSpec((1, TM, D_MODEL), lambda b, t: (b, jnp.maximum(t + tile0 - nct, 0), 0)),
            pl.BlockSpec((1, TM, D_MODEL), lambda b, t: (b, jnp.minimum(t + tile0, nct - 1), 0)),
            pl.BlockSpec((1, 1, D_MODEL),
                         lambda b, t: (layer * MOD_ROWS + jnp.where(t + tile0 < nct, MOD_ROWS - 8, b), 0, 2)),
            pl.BlockSpec((D_MODEL, D_MODEL), lambda b, t: (0, 0), pipeline_mode=pl.Buffered(1)),
            vec, vec,
        ],
        out_specs=out_specs,
        out_shape=out_shape,
        scratch_shapes=[vx, vx, vx, vx, pltpu.VMEM((4, TM, t_tot), F32), pltpu.VMEM((4, TM, t_tot), BF16)],
        compiler_params=pltpu.CompilerParams(dimension_semantics=("arbitrary", "arbitrary"),
                                             vmem_limit_bytes=56 * 1024 * 1024),
        name="attn_out",
    )(sink, at, at, lam, g2, ya, sg, x, cx, modr, wo, lng, lnb)
    return (res[0], res[1]) if ctx_out else (res[0], None)


def _in_perm():
    hp = np.array(HEAD_PERM)
    head_cols = (hp[:, None] * HEAD_DIM + np.arange(HEAD_DIM)[None, :]).reshape(-1)
    idx = np.arange(4 * GROUP_W + 2 * GROUP_W + 3 * GROUP_W + 2 * GROUP_W + D_MODEL)
    b0 = A_W
    d0 = A_W + 2 * GROUP_W + 3 * GROUP_W
    idx[b0:b0 + GROUP_W] = b0 + head_cols
    idx[d0:d0 + GROUP_W] = d0 + head_cols
    idx[GATE_OFF + GROUP_W:GATE_OFF + 2 * GROUP_W] = GATE_OFF + GROUP_W + head_cols
    idx[GATE_OFF + 3 * GROUP_W:GATE_OFF + 4 * GROUP_W] = GATE_OFF + 3 * GROUP_W + head_cols
    return idx


def _out_perm():
    hp = np.array(HEAD_PERM)
    head_rows = (hp[:, None] * HEAD_DIM + np.arange(HEAD_DIM)[None, :]).reshape(-1)
    idx = np.arange(D_MODEL)
    idx[GROUP_W:2 * GROUP_W] = GROUP_W + head_rows
    idx[3 * GROUP_W:4 * GROUP_W] = 3 * GROUP_W + head_rows
    return idx


def _take_runs(a, idx, axis):
    idx = [int(v) for v in idx]
    runs, start = [], idx[0]
    for prev, cur in zip(idx, idx[1:]):
        if cur != prev + 1:
            runs.append((start, prev + 1))
            start = cur
    runs.append((start, idx[-1] + 1))
    return jnp.concatenate([lax.slice_in_dim(a, lo, hi, axis=axis) for lo, hi in runs], axis=axis)


def _rope_tables(n_lat, n_ctx, dim):
    rows = n_lat // GRID_W
    row = jnp.repeat(jnp.arange(rows, dtype=F32), GRID_W)
    col = jnp.broadcast_to(jnp.arange(GRID_W, dtype=F32), (rows, GRID_W)).reshape(-1)
    d_axis = dim // 2
    inv = ROPE_THETA ** (-jnp.arange(0, d_axis, 2, dtype=F32) / d_axis)
    ang_r = row[:, None] * inv
    ang_c = col[:, None] * inv
    ang = jnp.concatenate([ang_r, ang_r, ang_c, ang_c], axis=-1)
    cos, sin = jnp.cos(ang), jnp.sin(ang)
    first = (np.arange(dim) % (dim // 2)) < (dim // 4)
    sa = jnp.where(first[None, :], -sin, 0.0)
    sb = jnp.where(first[None, :], 0.0, sin)
    reps = LANES // dim
    pad = lambda a, fill: jnp.concatenate([jnp.full((n_ctx, LANES), fill, F32), jnp.tile(a, (1, reps))], axis=0)
    return pad(cos, 1.0), pad(sa, 0.0), pad(sb, 0.0)


def kernel(x, c, ctx, c_ctx, w_in, w_out, w_ada, b_ada, ln_g, ln_b, hg_lb_logits, hg_norm_g, ga_q_norm_g,
           ga_k_norm_g, df_lambda, df_subln_g, wn_sink):
    bsz, s, d = x.shape
    lc = ctx.shape[1]
    depth = w_in.shape[0]
    assert d == D_MODEL and s % TM == 0 and lc % TM == 0 and s >= TM + 2 * WINDOW and bsz <= MOD_ROWS - 8
    assert w_in.shape[2] == IN_W

    cc = jnp.concatenate([c, jnp.zeros((MOD_ROWS - 8 - bsz, d), F32), c_ctx[None, :],
                          jnp.zeros((7, d), F32)], axis=0)
    mod = _ada_call(cc, w_ada, b_ada)
    modr = mod.reshape(depth * MOD_ROWS, 1, 3 * d)

    tabs = _rope_tables(s, lc, HEAD_DIM) + _rope_tables(s, lc, DF_QK)
    wp = _take_runs(w_in.astype(BF16), _in_perm(), 2)
    wo = _take_runs(w_out.astype(BF16), _out_perm(), 1)
    pmf = jnp.asarray(_hgrn_sum_matrix(False), BF16)
    pmb = jnp.asarray(_hgrn_sum_matrix(True), BF16)
    tile2 = lambda v: jnp.tile(v, 2)[None, :]
    alpha = (2.0 * depth) ** 0.25

    cx = ctx
    for l in range(depth):
        ctx_out = l < depth - 1
        za, at, sg = _inproj_call(x, cx, modr, l, wp[l], tile2(ga_q_norm_g[l]), tile2(ga_k_norm_g[l]), tabs)
        ya = _hgrn_call(za, hg_lb_logits, jnp.tile(hg_norm_g[l], 4)[None, :], pmf, pmb, l, lc)
        lam_init = 0.8 - 0.6 * math.exp(-0.3 * l)
        x, cx_new = _attn_out_call(at, ya, sg, x, cx, modr, l, wo[l], ln_g[l][None, :], ln_b[l][None, :], ctx_out,
                                   alpha, wn_sink[l], df_lambda[l], tile2(df_subln_g[l]), lam_init)
        cx = cx_new if ctx_out else cx
    return x
```
